```python
import jax, jax.numpy as jnp
from jax import lax
import numpy as np

D_MODEL = 1024
BATCH = 16
SEQ = 2048
DEPTH = 1
DEC_BATCH = 32
DEC_SEQ = 1
PAST_LEN = 16384
PAGE_SIZE = 128

N_MEM = 256
CHUNK = 128
Q_BLOCK = 128
ATT_HEADS = 8
ATT_HEAD_DIM = 64
ATT_WIDTH = ATT_HEADS * ATT_HEAD_DIM
SGU_HEADS = 8
SGU_HEAD_DIM = 64
SGU_WIDTH = SGU_HEADS * SGU_HEAD_DIM
MIX_WIDTH = ATT_WIDTH + SGU_WIDTH
O_Q = 0
O_K = ATT_WIDTH
O_V = 2 * ATT_WIDTH
O_F = 3 * ATT_WIDTH
O_U = O_F + ATT_HEADS
O_Z = O_U + SGU_WIDTH
N_IN = O_Z + SGU_WIDTH
MEM_HEADS = 4
MEM_HEAD_DIM = D_MODEL // MEM_HEADS
D_FF = 2816
CONV_W = 3
EPS = 1e-6

kernel_name = 'hybrid_fox_sgu_decoder_step'


def rmsnorm(x, g):
    xf = x.astype(jnp.float32)
    y = xf * lax.rsqrt(jnp.mean(xf * xf, axis=-1, keepdims=True) + EPS)
    return (y * g.astype(jnp.float32)).astype(x.dtype)


def mixer_inputs(xn, w_in, b_f, g_sgu_v):
    B, T, _ = xn.shape
    h = xn @ w_in
    q = h[..., O_Q:O_K].reshape(B, T, ATT_HEADS, ATT_HEAD_DIM)
    k = h[..., O_K:O_V].reshape(B, T, ATT_HEADS, ATT_HEAD_DIM)
    v = h[..., O_V:O_F].reshape(B, T, ATT_HEADS, ATT_HEAD_DIM)
    logf = jax.nn.log_sigmoid((h[..., O_F:O_U] + b_f).astype(jnp.float32))
    u = jax.nn.gelu(h[..., O_U:O_Z])
    z = rmsnorm(jax.nn.gelu(h[..., O_Z:]), g_sgu_v).reshape(B, T, SGU_HEADS, SGU_HEAD_DIM)
    return q, k, v, logf, u, z


def causal_chunk_weights(w_s):
    mask = jnp.tril(jnp.ones((CHUNK, CHUNK), dtype=bool))
    return jnp.where(mask, w_s, jnp.zeros((), w_s.dtype))


def sgu_prompt(u, z, w_s, b_s):
    B, T, H, hd = z.shape
    zc = z.reshape(B, T // CHUNK, CHUNK, H, hd)
    mixed = jnp.einsum('hts,bcshd->bcthd', causal_chunk_weights(w_s), zc) + b_s.T[None, None, :, :, None]
    return u * mixed.reshape(B, T, H * hd)


def sgu_sample(u, z, w_s, b_s):
    B, T, H, hd = z.shape
    wm = causal_chunk_weights(w_s)[:, :T, :T]
    mixed = jnp.einsum('hts,bshd->bthd', wm, z) + b_s[:, :T].T[None, :, :, None]
    return u * mixed.reshape(B, T, H * hd)


def fox_attend(q, k, v, cq, ck, qpos, kpos):
    s = jnp.einsum('bqhd,bkhd->bhqk', q, k).astype(jnp.float32) * (ATT_HEAD_DIM ** -0.5)
    s = s + (cq.astype(jnp.float32).transpose(0, 2, 1)[:, :, :, None]
             - ck.astype(jnp.float32).transpose(0, 2, 1)[:, :, None, :])
    s = jnp.where(kpos[None, :] <= qpos[:, None], s, -jnp.inf)
    p = jax.nn.softmax(s, axis=-1).astype(v.dtype)
    return jnp.einsum('bhqk,bkhd->bqhd', p, v)


def fox_prompt(q, k, v, logf):
    B, T, H, hd = q.shape
    c = jnp.cumsum(logf, axis=1)
    nb = T // Q_BLOCK
    qb = q.reshape(B, nb, Q_BLOCK, H, hd).swapaxes(0, 1)
    cqb = c.reshape(B, nb, Q_BLOCK, H).swapaxes(0, 1)
    starts = jnp.arange(nb, dtype=jnp.int32) * Q_BLOCK
    kpos = jnp.arange(T, dtype=jnp.int32)

    def one_block(args):
        qi, cqi, s0 = args
        return fox_attend(qi, k, v, cqi, c, s0 + jnp.arange(Q_BLOCK, dtype=jnp.int32), kpos)

    out = lax.map(one_block, (qb, cqb, starts))
    return out.swapaxes(0, 1).reshape(B, T, H * hd)


def fox_sample(q, k, v, logf, k_pool, v_pool, logf_pool, page_table):
    B, T, H, hd = q.shape
    past = page_table.shape[1] * PAGE_SIZE
    k_all = jnp.concatenate([k_pool[page_table].reshape(B, past, H, hd), k.astype(k_pool.dtype)], axis=1)
    v_all = jnp.concatenate([v_pool[page_table].reshape(B, past, H, hd), v.astype(v_pool.dtype)], axis=1)
    lf_all = jnp.concatenate([logf_pool[page_table].reshape(B, past, H).astype(jnp.float32), logf], axis=1)
    c = jnp.cumsum(lf_all, axis=1)
    qpos = past + jnp.arange(T, dtype=jnp.int32)
    kpos = jnp.arange(past + T, dtype=jnp.int32)
    out = fox_attend(q.astype(k_all.dtype), k_all, v_all, c[:, past:], c, qpos, kpos)
    return out.reshape(B, T, H * hd).astype(q.dtype)


def merge_heads(att, sgu, g_att_out, g_sgu_out, w_o):
    return jnp.concatenate([rmsnorm(att, g_att_out), rmsnorm(sgu, g_sgu_out)], axis=-1) @ w_o


def memory_kv(mem, g_mem, w_ck, w_cv):
    B, M, _ = mem.shape
    mn = rmsnorm(mem, g_mem)
    mk = (mn @ w_ck).reshape(B, M, MEM_HEADS, MEM_HEAD_DIM)
    mv = (mn @ w_cv).reshape(B, M, MEM_HEADS, MEM_HEAD_DIM)
    return mk, mv


def cross_attend(xn, mk, mv, w_cq, w_co):
    B, T, _ = xn.shape
    q = (xn @ w_cq).reshape(B, T, MEM_HEADS, MEM_HEAD_DIM).astype(mk.dtype)
    s = jnp.einsum('bqhd,bmhd->bhqm', q, mk).astype(jnp.float32) * (MEM_HEAD_DIM ** -0.5)
    p = jax.nn.softmax(s, axis=-1).astype(mv.dtype)
    o = jnp.einsum('bhqm,bmhd->bqhd', p, mv).reshape(B, T, MEM_HEADS * MEM_HEAD_DIM)
    return (o @ w_co).astype(xn.dtype)


def conv_ffn(xn, prev, w_up, conv_w, conv_b, w_down):
    T = xn.shape[1]
    h = xn @ w_up
    hp = jnp.concatenate([prev.astype(h.dtype), h], axis=1)
    hc = conv_b
    for i in range(CONV_W):
        hc = hc + conv_w[i] * hp[:, i:i + T]
    gate, val = jnp.split(hc, 2, axis=-1)
    return (jax.nn.silu(gate) * val) @ w_down, hp[:, T:]


def setup_inputs(seed: int = 0) -> dict:
    key = jax.random.key(seed)
    ks = jax.random.split(key, 40)
    n_pages = PAST_LEN // PAGE_SIZE
    n_used = DEC_BATCH * n_pages
    n_phys = n_used + n_used // 4

    def nrm(k, shape, scale):
        return jax.random.normal(k, shape, jnp.float32) * scale

    page_table = jax.random.permutation(ks[0], n_phys)[:n_used].reshape(DEC_BATCH, n_pages).astype(jnp.int32)
    return {
        'x_prompt': nrm(ks[1], (BATCH, SEQ, D_MODEL), 1.0),
        'x_sample': nrm(ks[2], (DEC_BATCH, DEC_SEQ, D_MODEL), 1.0),
        'cache_k': nrm(ks[3], (DEPTH, n_phys, PAGE_SIZE, ATT_HEADS, ATT_HEAD_DIM), 1.0),
        'cache_v': nrm(ks[4], (DEPTH, n_phys, PAGE_SIZE, ATT_HEADS, ATT_HEAD_DIM), 1.0),
        'cache_logf': jax.nn.log_sigmoid(2.0 + nrm(ks[5], (DEPTH, n_phys, PAGE_SIZE, ATT_HEADS), 1.0)),
        'cache_mem_k': nrm(ks[6], (DEPTH, DEC_BATCH, N_MEM, MEM_HEADS, MEM_HEAD_DIM), 1.0),
        'cache_mem_v': nrm(ks[7], (DEPTH, DEC_BATCH, N_MEM, MEM_HEADS, MEM_HEAD_DIM), 1.0),
        'state_conv': nrm(ks[8], (DEPTH, DEC_BATCH, CONV_W - 1, 2 * D_FF), 1.0),
        'page_table': page_table,
        'mem_prompt': nrm(ks[9], (BATCH, N_MEM, D_MODEL), 1.0),
        'g_mix': 1.0 + nrm(ks[10], (DEPTH, D_MODEL), 0.05),
        'w_in': nrm(ks[11], (DEPTH, D_MODEL, N_IN), D_MODEL ** -0.5),
        'b_f': 2.0 + nrm(ks[12], (DEPTH, ATT_HEADS), 0.1),
        'g_sgu_v': 1.0 + nrm(ks[13], (DEPTH, SGU_WIDTH), 0.05),
        'w_s': nrm(ks[14], (DEPTH, SGU_HEADS, CHUNK, CHUNK), CHUNK ** -0.5),
        'b_s': 1.0 + nrm(ks[15], (DEPTH, SGU_HEADS, CHUNK), 0.01),
        'g_att_out': 1.0 + nrm(ks[16], (DEPTH, ATT_WIDTH), 0.05),
        'g_sgu_out': 1.0 + nrm(ks[17], (DEPTH, SGU_WIDTH), 0.05),
        'w_o': nrm(ks[18], (DEPTH, MIX_WIDTH, D_MODEL), MIX_WIDTH ** -0.5),
        'g_cross': 1.0 + nrm(ks[19], (DEPTH, D_MODEL), 0.05),
        'g_mem': 1.0 + nrm(ks[20], (DEPTH, D_MODEL), 0.05),
        'w_cq': nrm(ks[21], (DEPTH, D_MODEL, D_MODEL), D_MODEL ** -0.5),
        'w_ck': nrm(ks[22], (DEPTH, D_MODEL, D_MODEL), D_MODEL ** -0.5),
        'w_cv': nrm(ks[23], (DEPTH, D_MODEL, D_MODEL), D_MODEL ** -0.5),
        'w_co': nrm(ks[24], (DEPTH, D_MODEL, D_MODEL), D_MODEL ** -0.5),
        'g_ffn': 1.0 + nrm(ks[25], (DEPTH, D_MODEL), 0.05),
        'w_up': nrm(ks[26], (DEPTH, D_MODEL, 2 * D_FF), D_MODEL ** -0.5),
        'conv_w': nrm(ks[27], (DEPTH, CONV_W, 2 * D_FF), CONV_W ** -0.5),
        'conv_b': nrm(ks[28], (DEPTH, 2 * D_FF), 0.01),
        'w_down': nrm(ks[29], (DEPTH, D_FF, D_MODEL), D_FF ** -0.5),
        'g_final': 1.0 + nrm(ks[30], (D_MODEL,), 0.05),
    }


def reference(x_prompt, x_sample, cache_k, cache_v, cache_logf, cache_mem_k, cache_mem_v, state_conv,
              page_table, mem_prompt, g_mix, w_in, b_f, g_sgu_v, w_s, b_s, g_att_out, g_sgu_out, w_o,
              g_cross, g_mem, w_cq, w_ck, w_cv, w_co, g_ffn, w_up, conv_w, conv_b, w_down, g_final):
    xp = x_prompt
    xs = x_sample
    bp = xp.shape[0]
    kp_l, vp_l, lfp_l, mkp_l, mvp_l, cvp_l = [], [], [], [], [], []
    ks_l, vs_l, lfs_l, zs_l, cvs_l = [], [], [], [], []
    for l in range(DEPTH):
        q, k, v, logf, u, z = mixer_inputs(rmsnorm(xp, g_mix[l]), w_in[l], b_f[l], g_sgu_v[l])
        xp = xp + merge_heads(fox_prompt(q, k, v, logf), sgu_prompt(u, z, w_s[l], b_s[l]),
                              g_att_out[l], g_sgu_out[l], w_o[l])
        mk, mv = memory_kv(mem_prompt, g_mem[l], w_ck[l], w_cv[l])
        xp = xp + cross_attend(rmsnorm(xp, g_cross[l]), mk, mv, w_cq[l], w_co[l])
        f, conv_p = conv_ffn(rmsnorm(xp, g_ffn[l]), jnp.zeros((bp, CONV_W - 1, 2 * D_FF), xp.dtype),
                             w_up[l], conv_w[l], conv_b[l], w_down[l])
        xp = xp + f
        kp_l.append(k); vp_l.append(v); lfp_l.append(logf)
        mkp_l.append(mk); mvp_l.append(mv); cvp_l.append(conv_p)

        q, k, v, logf, u, z = mixer_inputs(rmsnorm(xs, g_mix[l]), w_in[l], b_f[l], g_sgu_v[l])
        att = fox_sample(q, k, v, logf, cache_k[l], cache_v[l], cache_logf[l], page_table)
        xs = xs + merge_heads(att, sgu_sample(u, z, w_s[l], b_s[l]), g_att_out[l], g_sgu_out[l], w_o[l])
        xs = xs + cross_attend(rmsnorm(xs, g_cross[l]), cache_mem_k[l], cache_mem_v[l], w_cq[l], w_co[l])
        f, conv_s = conv_ffn(rmsnorm(xs, g_ffn[l]), state_conv[l], w_up[l], conv_w[l], conv_b[l], w_down[l])
        xs = xs + f
        ks_l.append(k); vs_l.append(v); lfs_l.append(logf); zs_l.append(z); cvs_l.append(conv_s)

    y_prompt = rmsnorm(xp, g_final)
    y_sample = rmsnorm(xs, g_final)
    k_prompt = jnp.stack(kp_l)
    v_prompt = jnp.stack(vp_l)
    logf_prompt = jnp.stack(lfp_l)
    mem_k_prompt = jnp.stack(mkp_l)
    mem_v_prompt = jnp.stack(mvp_l)
    conv_prompt = jnp.stack(cvp_l)
    k_sample = jnp.stack(ks_l)
    v_sample = jnp.stack(vs_l)
    logf_sample = jnp.stack(lfs_l)
    chunk_v_sample = jnp.stack(zs_l)
    conv_sample = jnp.stack(cvs_l)
    return (y_prompt, y_sample, k_prompt, v_prompt, logf_prompt, mem_k_prompt, mem_v_prompt, conv_prompt,
            k_sample, v_sample, logf_sample, chunk_v_sample, conv_sample)
```

```python
import functools
import math

import jax
import jax.numpy as jnp
from jax import lax
from jax.experimental import pallas as pl
from jax.experimental.pallas import tpu as pltpu

D_MODEL = 1024
PAGE_SIZE = 128
CHUNK = 128
ATT_HEADS = 8
ATT_HEAD_DIM = 64
ATT_WIDTH = ATT_HEADS * ATT_HEAD_DIM
SGU_WIDTH = 512
O_F = 3 * ATT_WIDTH
O_U = O_F + ATT_HEADS
N_MEM = 256
MEM_HEADS = 4
MEM_HEAD_DIM = D_MODEL // MEM_HEADS
D_FF = 2816
CONV_W = 3
EPS = 1e-6

LANES = 128
W_IN_COLS = 5 * 512 + LANES
VMEM_LIMIT = 56 * 1024 * 1024

F32 = jnp.float32
BF16 = jnp.bfloat16


def _rms(x, g):
    return x * lax.rsqrt(jnp.mean(x * x, axis=-1, keepdims=True) + EPS) * g


def _gelu(x):
    c = math.sqrt(2.0 / math.pi)
    return x * (0.5 * (1.0 + jnp.tanh(c * (x + 0.044715 * (x * x * x)))))


def _log_sigmoid(x):
    return jnp.minimum(x, 0.0) - jnp.log1p(jnp.exp(-jnp.abs(x)))


def _silu(x):
    return x * (1.0 / (1.0 + jnp.exp(-x)))


def _dot(a, b):
    return jnp.dot(a, b, preferred_element_type=F32)


def _dot_nt(a, b):
    return lax.dot_general(a, b, (((1,), (1,)), ((), ())), preferred_element_type=F32)


def _split3(x):
    hi = x.astype(BF16)
    r = x - hi.astype(F32)
    mid = r.astype(BF16)
    lo = (r - mid.astype(F32)).astype(BF16)
    return hi, mid, lo


def _params(*sem):
    return pltpu.CompilerParams(dimension_semantics=sem, vmem_limit_bytes=VMEM_LIMIT)


def _const(shape):
    return pl.BlockSpec(shape, lambda *_: (0,) * len(shape))


def _inproj_body(*refs, prompt, tm, tiles_per_batch):
    if prompt:
        (x_ref, g_ref, w_ref, bf_ref, gz_ref, tri_ref,
         qb_ref, k_ref, v_ref, kb_ref, vb_ref, lf_ref, u_ref, zb_ref, ct_ref, carry_ref) = refs
    else:
        (x_ref, g_ref, w_ref, bf_ref, gz_ref,
         q_ref, k_ref, v_ref, lf_ref, u_ref, z_ref) = refs
    xn = _rms(x_ref[...], g_ref[...]).astype(BF16)

    def proj(a, b):
        return _dot(xn, w_ref[:, a:b])

    q = proj(0, 512)
    k = proj(512, 1024)
    v = proj(1024, 1536)
    k_ref[...] = k
    v_ref[...] = v
    u_ref[...] = _gelu(proj(1536, 2048))
    z = _rms(_gelu(proj(2048, 2560)), gz_ref[...])
    lf = _log_sigmoid(proj(2560, W_IN_COLS) + bf_ref[...])
    lf_ref[...] = lf[:, :ATT_HEADS]
    if not prompt:
        q_ref[...] = q
        z_ref[...] = z
        return
    qb_ref[...] = (q * (ATT_HEAD_DIM ** -0.5)).astype(BF16)
    kb_ref[...] = k.astype(BF16)
    vb_ref[...] = v.astype(BF16)
    zb_ref[...] = z.astype(BF16)

    @pl.when(pl.program_id(0) % tiles_per_batch == 0)
    def _():
        carry_ref[...] = jnp.zeros_like(carry_ref)

    lft = lf.T[:ATT_HEADS, :]
    tri = tri_ref[...]
    hi, mid, lo = _split3(lft)
    c = _dot(hi, tri) + _dot(mid, tri) + _dot(lo, tri) + carry_ref[:, 0:1]
    ct_ref[...] = c
    carry_ref[...] = jnp.broadcast_to(c[:, tm - 1:tm], carry_ref.shape)


def _inproj(x2d, g, w_r, bf128, gz, tri, *, prompt, tm, tiles_per_batch):
    rows = x2d.shape[0]
    row = lambda n: pl.BlockSpec((tm, n), lambda i: (i, 0))
    in_specs = [row(D_MODEL), _const((1, D_MODEL)), _const((D_MODEL, W_IN_COLS)),
                _const((1, LANES)), _const((1, SGU_WIDTH))]
    args = [x2d, g, w_r, bf128, gz]
    sds = jax.ShapeDtypeStruct
    if prompt:
        in_specs.append(_const((tm, tm)))
        args.append(tri)
        out_shape = [sds((rows, 512), BF16), sds((rows, 512), F32), sds((rows, 512), F32),
                     sds((rows, 512), BF16), sds((rows, 512), BF16), sds((rows, ATT_HEADS), F32),
                     sds((rows, 512), F32), sds((rows, 512), BF16), sds((ATT_HEADS, rows), F32)]
        out_specs = [row(512)] * 5 + [row(ATT_HEADS), row(512), row(512),
                                      pl.BlockSpec((ATT_HEADS, tm), lambda i: (0, i))]
        scratch = [pltpu.VMEM((ATT_HEADS, LANES), F32)]
    else:
        out_shape = [sds((rows, 512), F32)] * 3 + [sds((rows, ATT_HEADS), F32)] + [sds((rows, 512), F32)] * 2
        out_specs = [row(512)] * 3 + [row(ATT_HEADS), row(512), row(512)]
        scratch = []
    return pl.pallas_call(
        functools.partial(_inproj_body, prompt=prompt, tm=tm, tiles_per_batch=tiles_per_batch),
        grid=(rows // tm,), in_specs=in_specs, out_specs=out_specs, out_shape=out_shape,
        scratch_shapes=scratch, compiler_params=_params("arbitrary"),
        name="inproj_prompt" if prompt else "inproj_sample")(*args)


def _fox_body(q_ref, k_ref, v_ref, c_ref, o_ref, *, tq):
    qi = pl.program_id(2)
    q = q_ref[...]
    lane = lax.broadcasted_iota(jnp.int32, (tq, LANES), 1)
    first = lane < ATT_HEAD_DIM
    zero = jnp.zeros_like(q)
    q_heads = (jnp.where(first, q, zero), jnp.where(first, zero, q))
    rowi = lax.broadcasted_iota(jnp.int32, (tq, tq), 0)
    coli = lax.broadcasted_iota(jnp.int32, (tq, tq), 1)
    causal = coli <= rowi

    def step(j, carry, diag):
        start = pl.multiple_of(j * tq, tq)
        kb = k_ref[pl.ds(start, tq), :]
        vb = v_ref[pl.ds(start, tq), :]
        out = []
        for hh in range(2):
            m, l, acc = carry[hh]
            s = _dot_nt(q_heads[hh], kb) - c_ref[0, hh, 0, pl.ds(j, 1), :]
            if diag:
                s = jnp.where(causal, s, -jnp.inf)
            m_new = jnp.maximum(m, jnp.max(s, axis=1, keepdims=True))
            p = jnp.exp(s - m_new)
            alpha = jnp.exp(m - m_new)
            l = alpha * l + jnp.sum(p, axis=1, keepdims=True)
            acc = alpha * acc + _dot(p.astype(BF16), vb)
            out.append((m_new, l, acc))
        return tuple(out)

    init = tuple((jnp.full((tq, 1), -jnp.inf, F32), jnp.zeros((tq, 1), F32),
                  jnp.zeros((tq, LANES), F32)) for _ in range(2))
    carry = lax.fori_loop(0, qi, lambda j, c: step(j, c, False), init)
    (_, la, acca), (_, lb, accb) = step(qi, carry, True)
    o_ref[...] = jnp.where(first, acca / la, accb / lb)


def _fox_prompt(qb, kb, vb, c5, *, batch, seq, tq):
    nq = seq // tq
    pairs = ATT_HEADS // 2
    return pl.pallas_call(
        functools.partial(_fox_body, tq=tq),
        grid=(batch, pairs, nq),
        in_specs=[pl.BlockSpec((tq, LANES), lambda b, p, i: (b * nq + i, p)),
                  pl.BlockSpec((seq, LANES), lambda b, p, i: (b, p)),
                  pl.BlockSpec((seq, LANES), lambda b, p, i: (b, p)),
                  pl.BlockSpec((1, 2, 1, nq, tq), lambda b, p, i: (p, 0, b, 0, 0))],
        out_specs=pl.BlockSpec((tq, LANES), lambda b, p, i: (b * nq + i, p)),
        out_shape=jax.ShapeDtypeStruct((batch * seq, ATT_WIDTH), F32),
        compiler_params=_params("arbitrary", "arbitrary", "arbitrary"),
        name="fox_prompt")(qb, kb, vb, c5)


def _merge_body(att_ref, u_ref, z_ref, x_ref, ws_ref, bs_ref, ga_ref, gs_ref, wo_ref,
                o_ref, sgu_ref, *, tm):
    rowi = lax.broadcasted_iota(jnp.int32, (CHUNK, CHUNK), 0)
    coli = lax.broadcasted_iota(jnp.int32, (CHUNK, CHUNK), 1)
    tril = coli <= rowi
    wm = [jnp.where(tril, ws_ref[h], 0.0).astype(BF16) for h in range(ATT_HEADS)]
    first = lax.broadcasted_iota(jnp.int32, (CHUNK, LANES), 1) < ATT_HEAD_DIM
    for c in range(tm // CHUNK):
        r0, r1 = c * CHUNK, (c + 1) * CHUNK
        for p in range(SGU_WIDTH // LANES):
            c0, c1 = p * LANES, (p + 1) * LANES
            zp = z_ref[r0:r1, c0:c1]
            mixed = jnp.where(first, _dot(wm[2 * p], zp), _dot(wm[2 * p + 1], zp)) + bs_ref[:, c0:c1]
            sgu_ref[r0:r1, c0:c1] = u_ref[r0:r1, c0:c1] * mixed
    an = _rms(att_ref[...], ga_ref[...]).astype(BF16)
    sn = _rms(sgu_ref[...], gs_ref[...]).astype(BF16)
    o_ref[...] = x_ref[...] + _dot(an, wo_ref[0:ATT_WIDTH, :]) + _dot(sn, wo_ref[ATT_WIDTH:, :])


def _merge_prompt(att, u, zb, x2d, ws, bs_full, ga, gs, wo, *, tm):
    rows = x2d.shape[0]
    row = lambda n: pl.BlockSpec((tm, n), lambda i: (i, 0))
    return pl.pallas_call(
        functools.partial(_merge_body, tm=tm),
        grid=(rows // tm,),
        in_specs=[row(512), row(512), row(512), row(D_MODEL),
                  _const((ATT_HEADS, CHUNK, CHUNK)), _const((CHUNK, SGU_WIDTH)),
                  _const((1, 512)), _const((1, 512)), _const((D_MODEL, D_MODEL))],
        out_specs=row(D_MODEL),
        out_shape=jax.ShapeDtypeStruct((rows, D_MODEL), F32),
        scratch_shapes=[pltpu.VMEM((tm, SGU_WIDTH), F32)],
        compiler_params=_params("arbitrary"), name="merge_prompt")(att, u, zb, x2d, ws, bs_full, ga, gs, wo)


def _memkv_body(m_ref, g_ref, wk_ref, wv_ref, k_ref, v_ref, kb_ref, vb_ref):
    mn = _rms(m_ref[...], g_ref[...]).astype(BF16)
    k = _dot(mn, wk_ref[...])
    v = _dot(mn, wv_ref[...])
    k_ref[...] = k
    v_ref[...] = v
    kb_ref[...] = k.astype(BF16)
    vb_ref[...] = v.astype(BF16)


def _memkv(mem2d, g, wk, wv, *, tm):
    rows = mem2d.shape[0]
    row = pl.BlockSpec((tm, D_MODEL), lambda i: (i, 0))
    sds = jax.ShapeDtypeStruct
    return pl.pallas_call(
        _memkv_body, grid=(rows // tm,),
        in_specs=[row, _const((1, D_MODEL)), _const((D_MODEL, D_MODEL)), _const((D_MODEL, D_MODEL))],
        out_specs=[row] * 4,
        out_shape=[sds((rows, D_MODEL), F32)] * 2 + [sds((rows, D_MODEL), BF16)] * 2,
        compiler_params=_params("arbitrary"), name="memkv")(mem2d, g, wk, wv)


def _cross_body(x_ref, mk_ref, mv_ref, g_ref, wq_ref, wo_ref, o_ref, oc_ref):
    x = x_ref[...]
    xn = _rms(x, g_ref[...]).astype(BF16)
    q = (_dot(xn, wq_ref[...]) * (MEM_HEAD_DIM ** -0.5)).astype(BF16)
    for h in range(MEM_HEADS):
        c0, c1 = h * MEM_HEAD_DIM, (h + 1) * MEM_HEAD_DIM
        s = _dot_nt(q[:, c0:c1], mk_ref[:, c0:c1])
        p = jnp.exp(s - jnp.max(s, axis=1, keepdims=True))
        l = jnp.sum(p, axis=1, keepdims=True)
        oc_ref[:, c0:c1] = (_dot(p.astype(BF16), mv_ref[:, c0:c1]) / l).astype(BF16)
    o_ref[...] = x + _dot(oc_ref[...], wo_ref[...])


def _cross_prompt(x2d, mkb, mvb, g, wq, wo, *, batch, seq, tm):
    nt = seq // tm
    row = pl.BlockSpec((tm, D_MODEL), lambda b, i: (b * nt + i, 0))
    mem = pl.BlockSpec((N_MEM, D_MODEL), lambda b, i: (b, 0))
    return pl.pallas_call(
        _cross_body, grid=(batch, nt),
        in_specs=[row, mem, mem, _const((1, D_MODEL)), _const((D_MODEL, D_MODEL)), _const((D_MODEL, D_MODEL))],
        out_specs=row, out_shape=jax.ShapeDtypeStruct(x2d.shape, F32),
        scratch_shapes=[pltpu.VMEM((tm, D_MODEL), BF16)],
        compiler_params=_params("arbitrary", "arbitrary"), name="cross_prompt")(x2d, mkb, mvb, g, wq, wo)


HALO = 16
FF_TILE = 256


def _ffn_body(x_ref, xp_ref, g_ref, wu_ref, cw_ref, cb_ref, wd_ref, gf_ref,
              y_ref, cg_ref, cv_ref, hg_ref, hv_ref, acc_ref, *, tm, tiles_per_batch):
    first = pl.program_id(0) % tiles_per_batch == 0
    x = x_ref[...]
    g = g_ref[...]
    xn = _rms(x, g).astype(BF16)
    xpn = _rms(xp_ref[...], g).astype(BF16)
    acc_ref[...] = x
    for c in range(D_FF // FF_TILE):
        a, b = c * FF_TILE, (c + 1) * FF_TILE
        halves = []
        for off, h_ref, c_ref in ((0, hg_ref, cg_ref), (D_FF, hv_ref, cv_ref)):
            w = wu_ref[:, off + a:off + b]
            hp = _dot(xpn, w)
            h_ref[0:HALO, :] = jnp.where(first, jnp.zeros_like(hp), hp)
            h_ref[HALO:HALO + tm, :] = _dot(xn, w)
            cw = cw_ref[:, off + a:off + b]
            halves.append(cb_ref[:, off + a:off + b]
                          + cw[0:1] * h_ref[HALO - 2:HALO - 2 + tm, :]
                          + cw[1:2] * h_ref[HALO - 1:HALO - 1 + tm, :]
                          + cw[2:3] * h_ref[HALO:HALO + tm, :])
            c_ref[0, :, a:b] = h_ref[HALO + tm - 2:HALO + tm, :]
        act = (_silu(halves[0]) * halves[1]).astype(BF16)
        acc_ref[...] += _dot(act, wd_ref[a:b, :])
    y_ref[...] = _rms(acc_ref[...], gf_ref[...])


def _ffn_prompt(x2d, g, wu, cw, cb, wd, gf, *, batch, seq, tm):
    rows = x2d.shape[0]
    tpb = seq // tm
    hb = tm // HALO
    single = dict(pipeline_mode=pl.Buffered(1))
    sds = jax.ShapeDtypeStruct
    conv_spec = pl.BlockSpec((1, CONV_W - 1, D_FF), lambda i: (i // tpb, 0, 0))
    return pl.pallas_call(
        functools.partial(_ffn_body, tm=tm, tiles_per_batch=tpb),
        grid=(rows // tm,),
        in_specs=[pl.BlockSpec((tm, D_MODEL), lambda i: (i, 0)),
                  pl.BlockSpec((HALO, D_MODEL), lambda i: (jnp.maximum(i * hb - 1, 0), 0)),
                  _const((1, D_MODEL)),
                  pl.BlockSpec((D_MODEL, 2 * D_FF), lambda i: (0, 0), **single),
                  _const((CONV_W, 2 * D_FF)), _const((1, 2 * D_FF)),
                  pl.BlockSpec((D_FF, D_MODEL), lambda i: (0, 0), **single),
                  _const((1, D_MODEL))],
        out_specs=[pl.BlockSpec((tm, D_MODEL), lambda i: (i, 0)), conv_spec, conv_spec],
        out_shape=[sds((rows, D_MODEL), F32), sds((batch, CONV_W - 1, D_FF), F32),
                   sds((batch, CONV_W - 1, D_FF), F32)],
        scratch_shapes=[pltpu.VMEM((HALO + tm, FF_TILE), F32), pltpu.VMEM((HALO + tm, FF_TILE), F32),
                        pltpu.VMEM((tm, D_MODEL), F32)],
        compiler_params=_params("arbitrary"), name="ffn_prompt")(x2d, x2d, g, wu, cw, cb, wd, gf)


DEC_PAGES_PER_STEP = 8
HD_ROWS = ATT_HEADS * ATT_HEAD_DIM


def _head_rows(x):
    n = x.shape[-1]
    return jnp.broadcast_to(x[:, None, :], (ATT_HEADS, ATT_HEAD_DIM, n)).reshape(HD_ROWS, n)


def _head_sum(x):
    n = x.shape[-1]
    return jnp.sum(x.reshape(ATT_HEADS, ATT_HEAD_DIM, n), axis=1)


def _decode_body(pt_ref, q_ref, kn_ref, vn_ref, lfn_ref, *rest, npg):
    del pt_ref
    k_refs, v_refs, lf_refs = rest[:npg], rest[npg:2 * npg], rest[2 * npg:3 * npg]
    o_ref, m_ref, l_ref, acc_ref, carry_ref = rest[3 * npg:]
    j = pl.program_id(1)
    qcol = q_ref[0] * (ATT_HEAD_DIM ** -0.5)
    lane8 = lax.broadcasted_iota(jnp.int32, (ATT_HEADS, LANES), 1)

    @pl.when(j == 0)
    def _():
        s_new = _head_sum(qcol * kn_ref[0])
        m_ref[...] = jnp.broadcast_to(s_new, m_ref.shape)
        l_ref[...] = jnp.where(lane8 == 0, 1.0, 0.0)
        lane = lax.broadcasted_iota(jnp.int32, (HD_ROWS, LANES), 1)
        acc_ref[...] = jnp.where(lane == 0, jnp.broadcast_to(vn_ref[0], (HD_ROWS, LANES)), 0.0)
        carry_ref[...] = jnp.broadcast_to(lfn_ref[0], carry_ref.shape)

    qrep = jnp.broadcast_to(qcol, (HD_ROWS, LANES))
    carry = carry_ref[...]
    scores = []
    for i in range(npg):
        lf = lf_refs[i][0]
        inc = lf
        for sh in (1, 2, 4, 8, 16, 32, 64):
            inc = inc + jnp.where(lane8 < LANES - sh, pltpu.roll(inc, LANES - sh, 1), 0.0)
        scores.append(_head_sum(k_refs[i][0] * qrep) + (carry + (inc - lf)))
        carry = carry + jnp.broadcast_to(inc[:, 0:1], carry.shape)
    carry_ref[...] = carry

    mx = functools.reduce(jnp.maximum, scores)
    m_old = m_ref[...]
    m_new = jnp.maximum(m_old, jnp.broadcast_to(jnp.max(mx, axis=1, keepdims=True), m_old.shape))
    alpha = jnp.exp(m_old - m_new)
    m_ref[...] = m_new
    probs = [jnp.exp(s - m_new) for s in scores]
    l_ref[...] = alpha * l_ref[...] + functools.reduce(jnp.add, probs)
    acc = acc_ref[...] * _head_rows(alpha)
    for i in range(npg):
        acc = acc + v_refs[i][0] * _head_rows(probs[i])
    acc_ref[...] = acc

    @pl.when(j == pl.num_programs(1) - 1)
    def _():
        l = jnp.sum(l_ref[...], axis=1, keepdims=True)
        o_ref[0] = jnp.sum(acc, axis=1, keepdims=True) / _head_rows(l)


def _fox_decode(page_table, qcol, kncol, vncol, lfncol, kt, vt, lft):
    nb, npages = page_table.shape
    npg = DEC_PAGES_PER_STEP
    col = lambda n: pl.BlockSpec((1, n, 1), lambda b, j, pt: (b, 0, 0))

    def page(rows, i):
        return pl.BlockSpec((1, rows, PAGE_SIZE),
                            lambda b, j, pt: (pt[b, npages - 1 - (j * npg + i)], 0, 0))

    in_specs = ([col(HD_ROWS), col(HD_ROWS), col(HD_ROWS), col(ATT_HEADS)]
                + [page(HD_ROWS, i) for i in range(npg)] * 2
                + [page(ATT_HEADS, i) for i in range(npg)])
    grid_spec = pltpu.PrefetchScalarGridSpec(
        num_scalar_prefetch=1, grid=(nb, npages // npg), in_specs=in_specs,
        out_specs=col(HD_ROWS),
        scratch_shapes=[pltpu.VMEM((ATT_HEADS, LANES), F32), pltpu.VMEM((ATT_HEADS, LANES), F32),
                        pltpu.VMEM((HD_ROWS, LANES), F32), pltpu.VMEM((ATT_HEADS, LANES), F32)])
    return pl.pallas_call(
        functools.partial(_decode_body, npg=npg), grid_spec=grid_spec,
        out_shape=jax.ShapeDtypeStruct((nb, HD_ROWS, 1), F32),
        compiler_params=_params("arbitrary", "arbitrary"), name="fox_decode")(
            page_table, qcol, kncol, vncol, lfncol, *([kt] * npg), *([vt] * npg), *([lft] * npg))


def _merge_sample_body(att_ref, u_ref, z_ref, x_ref, w0_ref, b0_ref, ga_ref, gs_ref, wo_ref,
                       gc_ref, wq_ref, x1_ref, q_ref):
    sgu = u_ref[...] * (w0_ref[...] * z_ref[...] + b0_ref[...])
    an = _rms(att_ref[...], ga_ref[...]).astype(BF16)
    sn = _rms(sgu, gs_ref[...]).astype(BF16)
    x1 = x_ref[...] + _dot(an, wo_ref[0:ATT_WIDTH, :]) + _dot(sn, wo_ref[ATT_WIDTH:, :])
    x1_ref[...] = x1
    q_ref[...] = _dot(_rms(x1, gc_ref[...]).astype(BF16), wq_ref[...]) * (MEM_HEAD_DIM ** -0.5)


def _merge_sample(att, u, z, x2d, w0, b0, ga, gs, wo, gc, wq):
    sds = jax.ShapeDtypeStruct
    return pl.pallas_call(
        _merge_sample_body,
        out_shape=[sds(x2d.shape, F32), sds(x2d.shape, F32)],
        compiler_params=pltpu.CompilerParams(vmem_limit_bytes=VMEM_LIMIT),
        name="merge_sample")(att, u, z, x2d, w0, b0, ga, gs, wo, gc, wq)


def _cross_sample_body(q_ref, mk_ref, mv_ref, o_ref):
    q = q_ref[0]
    rowi = lax.broadcasted_iota(jnp.int32, (8, D_MODEL), 0)
    head = lax.broadcasted_iota(jnp.int32, (8, D_MODEL), 1) // MEM_HEAD_DIM
    own = rowi == head
    qbd = jnp.where(own, jnp.broadcast_to(q, (8, D_MODEL)), 0.0).astype(BF16)
    s = _dot_nt(qbd, mk_ref[0].astype(BF16))
    p = jnp.exp(s - jnp.max(s, axis=1, keepdims=True))
    l = jnp.sum(p, axis=1, keepdims=True)
    o = _dot(p.astype(BF16), mv_ref[0].astype(BF16)) / l
    o_ref[0] = jnp.sum(jnp.where(own, o, 0.0), axis=0, keepdims=True)


def _cross_sample(q3, mk, mv):
    nb = q3.shape[0]
    vec = pl.BlockSpec((1, 1, D_MODEL), lambda b: (b, 0, 0))
    mem = pl.BlockSpec((1, N_MEM, D_MODEL), lambda b: (b, 0, 0))
    return pl.pallas_call(
        _cross_sample_body, grid=(nb,), in_specs=[vec, mem, mem], out_specs=vec,
        out_shape=jax.ShapeDtypeStruct(q3.shape, F32),
        compiler_params=_params("arbitrary"), name="cross_sample")(q3, mk, mv)


def _ffn_sample_body(o_ref, x_ref, wco_ref, g_ref, wug_ref, wuv_ref, cwg_ref, cwv_ref, cbg_ref, cbv_ref,
                     p0g_ref, p0v_ref, p1g_ref, p1v_ref, wd_ref, gf_ref,
                     y_ref, hg_ref, hv_ref, x2_ref, xn_ref):
    j = pl.program_id(0)

    @pl.when(j == 0)
    def _():
        x2 = x_ref[...] + _dot(o_ref[...].astype(BF16), wco_ref[...])
        x2_ref[...] = x2
        xn_ref[...] = _rms(x2, g_ref[...]).astype(BF16)

    xn = xn_ref[...]
    hg = _dot(xn, wug_ref[...])
    hv = _dot(xn, wuv_ref[...])
    hg_ref[...] = hg
    hv_ref[...] = hv
    cwg = cwg_ref[...]
    cwv = cwv_ref[...]
    gate = cbg_ref[...] + cwg[0:1] * p0g_ref[...] + cwg[1:2] * p1g_ref[...] + cwg[2:3] * hg
    val = cbv_ref[...] + cwv[0:1] * p0v_ref[...] + cwv[1:2] * p1v_ref[...] + cwv[2:3] * hv
    x2_ref[...] += _dot((_silu(gate) * val).astype(BF16), wd_ref[...])

    @pl.when(j == pl.num_programs(0) - 1)
    def _():
        y_ref[...] = _rms(x2_ref[...], gf_ref[...])


def _ffn_sample(o, x1, wco, g, wu, cw, cb, prev0, prev1, wd, gf):
    nb = x1.shape[0]
    nf = D_FF // FF_TILE
    gcol = lambda r: pl.BlockSpec((r, FF_TILE), lambda j: (0, j))
    vcol = lambda r: pl.BlockSpec((r, FF_TILE), lambda j: (0, nf + j))
    sds = jax.ShapeDtypeStruct
    return pl.pallas_call(
        _ffn_sample_body, grid=(nf,),
        in_specs=[_const((nb, D_MODEL)), _const((nb, D_MODEL)), _const((D_MODEL, D_MODEL)), _const((1, D_MODEL)),
                  gcol(D_MODEL), vcol(D_MODEL), gcol(CONV_W), vcol(CONV_W), gcol(1), vcol(1),
                  gcol(nb), vcol(nb), gcol(nb), vcol(nb),
                  pl.BlockSpec((FF_TILE, D_MODEL), lambda j: (j, 0)), _const((1, D_MODEL))],
        out_specs=[_const((nb, D_MODEL)), gcol(nb), gcol(nb)],
        out_shape=[sds((nb, D_MODEL), F32), sds((nb, D_FF), F32), sds((nb, D_FF), F32)],
        scratch_shapes=[pltpu.VMEM((nb, D_MODEL), F32), pltpu.VMEM((nb, D_MODEL), BF16)],
        compiler_params=_params("arbitrary"), name="ffn_sample")(
            o, x1, wco, g, wu, wu, cw, cw, cb, cb, prev0, prev0, prev1, prev1, wd, gf)


PROMPT_TILE = 512
ATT_TILE = 256


def kernel(x_prompt, x_sample, cache_k, cache_v, cache_logf, cache_mem_k, cache_mem_v, state_conv,
           page_table, mem_prompt, g_mix, w_in, b_f, g_sgu_v, w_s, b_s, g_att_out, g_sgu_out, w_o,
           g_cross, g_mem, w_cq, w_ck, w_cv, w_co, g_ffn, w_up, conv_w, conv_b, w_down, g_final):
    depth = w_in.shape[0]
    bp, seq, _ = x_prompt.shape
    bs = x_sample.shape[0]
    assert depth == 1 and x_sample.shape[1] == 1 and seq % PROMPT_TILE == 0
    n_phys = cache_k.shape[1]
    tm = PROMPT_TILE
    tpb = seq // tm
    nq = seq // ATT_TILE
    row1 = lambda a: a.reshape(1, -1)

    xp = x_prompt.reshape(bp * seq, D_MODEL)
    xs = x_sample.reshape(bs, D_MODEL)
    tri = (lax.broadcasted_iota(jnp.int32, (tm, tm), 0)
           <= lax.broadcasted_iota(jnp.int32, (tm, tm), 1)).astype(BF16)
    gfin = row1(g_final)
    outs = [[] for _ in range(11)]
    for l in range(depth):
        wl = w_in[l]
        w_r = jnp.concatenate([wl[:, :O_F], wl[:, O_U:],
                               jnp.pad(wl[:, O_F:O_U], ((0, 0), (0, LANES - ATT_HEADS)))], axis=1).astype(BF16)
        bf128 = jnp.pad(b_f[l], (0, LANES - ATT_HEADS)).reshape(1, LANES)
        wo_b, wcq_b, wck_b, wcv_b, wco_b = (w[l].astype(BF16) for w in (w_o, w_cq, w_ck, w_cv, w_co))
        wu_b, wd_b = w_up[l].astype(BF16), w_down[l].astype(BF16)
        gmix, gz, ga, gs = row1(g_mix[l]), row1(g_sgu_v[l]), row1(g_att_out[l]), row1(g_sgu_out[l])
        gc, gm, gff = row1(g_cross[l]), row1(g_mem[l]), row1(g_ffn[l])
        cb = row1(conv_b[l])
        bs_full = jnp.repeat(b_s[l].T, ATT_HEAD_DIM, axis=1)

        qb, k, v, kb, vb, lf, u, zb, ct = _inproj(xp, gmix, w_r, bf128, gz, tri, prompt=True,
                                                  tm=tm, tiles_per_batch=tpb)
        c5 = ct.reshape(ATT_HEADS // 2, 2, bp, nq, ATT_TILE)
        att = _fox_prompt(qb, kb, vb, c5, batch=bp, seq=seq, tq=ATT_TILE)
        x1 = _merge_prompt(att, u, zb, xp, w_s[l], bs_full, ga, gs, wo_b, tm=tm)
        mk, mv, mkb, mvb = _memkv(mem_prompt.reshape(bp * N_MEM, D_MODEL), gm, wck_b, wcv_b, tm=tm)
        x2 = _cross_prompt(x1, mkb, mvb, gc, wcq_b, wco_b, batch=bp, seq=seq, tm=tm)
        xp, cg, cv = _ffn_prompt(x2, gff, wu_b, conv_w[l], cb, wd_b, gfin, batch=bp, seq=seq, tm=tm)
        outs[0].append(k.reshape(bp, seq, ATT_HEADS, ATT_HEAD_DIM))
        outs[1].append(v.reshape(bp, seq, ATT_HEADS, ATT_HEAD_DIM))
        outs[2].append(lf.reshape(bp, seq, ATT_HEADS))
        outs[3].append(mk.reshape(bp, N_MEM, MEM_HEADS, MEM_HEAD_DIM))
        outs[4].append(mv.reshape(bp, N_MEM, MEM_HEADS, MEM_HEAD_DIM))
        outs[5].append(jnp.concatenate([cg, cv], axis=-1))

        qs, ks, vs, lfs, us, zs = _inproj(xs, gmix, w_r, bf128, gz, None, prompt=False,
                                          tm=bs, tiles_per_batch=1)
        kt = jnp.transpose(cache_k[l], (0, 2, 3, 1)).reshape(n_phys, HD_ROWS, PAGE_SIZE)
        vt = jnp.transpose(cache_v[l], (0, 2, 3, 1)).reshape(n_phys, HD_ROWS, PAGE_SIZE)
        lft = jnp.transpose(cache_logf[l], (0, 2, 1))
        col = lambda a: a.reshape(bs, -1, 1)
        att_s = _fox_decode(page_table, col(qs), col(ks), col(vs), col(lfs), kt, vt, lft).reshape(bs, ATT_WIDTH)
        w0 = row1(jnp.repeat(w_s[l][:, 0, 0], ATT_HEAD_DIM))
        b0 = row1(jnp.repeat(b_s[l][:, 0], ATT_HEAD_DIM))
        x1s, qcs = _merge_sample(att_s, us, zs, xs, w0, b0, ga, gs, wo_b, gc, wcq_b)
        oc = _cross_sample(qcs.reshape(bs, 1, D_MODEL),
                           cache_mem_k[l].reshape(bs, N_MEM, D_MODEL),
                           cache_mem_v[l].reshape(bs, N_MEM, D_MODEL)).reshape(bs, D_MODEL)
        prev0, prev1 = state_conv[l][:, 0, :], state_conv[l][:, 1, :]
        xs, hg, hv = _ffn_sample(oc, x1s, wco_b, gff, wu_b, conv_w[l], cb, prev0, prev1, wd_b, gfin)
        h_new = jnp.concatenate([hg, hv], axis=-1)
        outs[6].append(ks.reshape(bs, 1, ATT_HEADS, ATT_HEAD_DIM))
        outs[7].append(vs.reshape(bs, 1, ATT_HEADS, ATT_HEAD_DIM))
        outs[8].append(lfs.reshape(bs, 1, ATT_HEADS))
        outs[9].append(zs.reshape(bs, 1, ATT_HEADS, ATT_HEAD_DIM))
        outs[10].append(jnp.stack([prev1, h_new], axis=1))

    y_prompt = xp.reshape(bp, seq, D_MODEL)
    y_sample = xs.reshape(bs, 1, D_MODEL)
    return (y_prompt, y_sample) + tuple(jnp.stack(o) for o in outs)
```

```python
import functools
import math

import jax
import jax.numpy as jnp
from jax import lax
from jax.experimental import pallas as pl
from jax.experimental.pallas import tpu as pltpu

D_MODEL = 1024
PAGE_SIZE = 128
CHUNK = 128
ATT_HEADS = 8
ATT_HEAD_DIM = 64
ATT_WIDTH = ATT_HEADS * ATT_HEAD_DIM
SGU_WIDTH = 512
O_F = 3 * ATT_WIDTH
O_U = O_F + ATT_HEADS
N_MEM = 256
MEM_HEADS = 4
MEM_HEAD_DIM = D_MODEL // MEM_HEADS
D_FF = 2816
CONV_W = 3
EPS = 1e-6

LOG2E = math.log2(math.e)
LANES = 128
W_IN_COLS = 5 * 512 + LANES
VMEM_LIMIT = 56 * 1024 * 1024

F32 = jnp.float32
BF16 = jnp.bfloat16


def _rms(x, g):
    return x * lax.rsqrt(jnp.mean(x * x, axis=-1, keepdims=True) + EPS) * g


def _gelu(x):
    c = math.sqrt(2.0 / math.pi)
    return x * (0.5 * (1.0 + jnp.tanh(c * (x + 0.044715 * (x * x * x)))))


def _log_sigmoid(x):
    return jnp.minimum(x, 0.0) - jnp.log1p(jnp.exp(-jnp.abs(x)))


def _silu(x):
    return x * (1.0 / (1.0 + jnp.exp(-x)))


def _dot(a, b):
    return jnp.dot(a, b, preferred_element_type=F32)


def _dot_nt(a, b):
    return lax.dot_general(a, b, (((1,), (1,)), ((), ())), preferred_element_type=F32)


def _split3(x):
    hi = x.astype(BF16)
    r = x - hi.astype(F32)
    mid = r.astype(BF16)
    lo = (r - mid.astype(F32)).astype(BF16)
    return hi, mid, lo


def _params(*sem):
    return pltpu.CompilerParams(dimension_semantics=sem, vmem_limit_bytes=VMEM_LIMIT)


def _const(shape):
    return pl.BlockSpec(shape, lambda *_: (0,) * len(shape))


def _inproj_body(*refs, prompt, tm, tiles_per_batch):
    if prompt:
        (x_ref, g_ref, w_ref, bf_ref, gz_ref, tri_ref,
         qt_ref, kt_ref, vt_ref, kb_ref, aug_ref, lft_ref, u_ref, zb_ref, carry_ref) = refs
    else:
        (x_ref, g_ref, w_ref, bf_ref, gz_ref,
         q_ref, k_ref, v_ref, lf_ref, u_ref, z_ref) = refs
    xn = _rms(x_ref[...], g_ref[...]).astype(BF16)

    def proj(a, b):
        return _dot(xn, w_ref[:, a:b])

    q = proj(0, 512)
    k = proj(512, 1024)
    v = proj(1024, 1536)
    u_ref[...] = _gelu(proj(1536, 2048))
    z = _rms(_gelu(proj(2048, 2560)), gz_ref[...])
    lf = _log_sigmoid(proj(2560, W_IN_COLS) + bf_ref[...])
    if not prompt:
        q_ref[...] = q
        k_ref[...] = k
        v_ref[...] = v
        lf_ref[...] = lf[:, :ATT_HEADS]
        z_ref[...] = z
        return
    qt_ref[...] = (q * (ATT_HEAD_DIM ** -0.5 * LOG2E)).T.astype(BF16)
    kt_ref[0] = k.T
    vt_ref[0] = v.T
    kb_ref[...] = k.astype(BF16)
    zb_ref[...] = z.astype(BF16)

    @pl.when(pl.program_id(0) % tiles_per_batch == 0)
    def _():
        carry_ref[...] = jnp.zeros_like(carry_ref)

    lft = lf.T[:ATT_HEADS, :]
    lft_ref[0] = lft
    tri = tri_ref[...]
    hi, mid, lo = _split3(lft)
    c = _dot(hi, tri) + _dot(mid, tri) + _dot(lo, tri) + carry_ref[:, 0:1]
    carry_ref[...] = jnp.broadcast_to(c[:, tm - 1:tm], carry_ref.shape)
    terms = [t.astype(F32) for t in _split3(c * (-LOG2E))]
    pad = jnp.zeros((LANES - 3 * ATT_HEADS, tm), F32)
    aug_ref[...] = jnp.concatenate(terms + [pad], axis=0).T.astype(BF16)


def _inproj(x2d, g, w_r, bf128, gz, tri, *, prompt, tm, tiles_per_batch):
    rows = x2d.shape[0]
    row = lambda n: pl.BlockSpec((tm, n), lambda i: (i, 0))
    in_specs = [row(D_MODEL), _const((1, D_MODEL)), _const((D_MODEL, W_IN_COLS)),
                _const((1, LANES)), _const((1, SGU_WIDTH))]
    args = [x2d, g, w_r, bf128, gz]
    sds = jax.ShapeDtypeStruct
    if prompt:
        in_specs.append(_const((tm, tm)))
        args.append(tri)
        nb = rows // (tm * tiles_per_batch)
        seq = tm * tiles_per_batch
        tpb = tiles_per_batch
        tl = lambda n: pl.BlockSpec((1, n, tm), lambda i: (i // tpb, 0, i % tpb))
        out_shape = [sds((512, rows), BF16), sds((nb, 512, seq), F32), sds((nb, 512, seq), F32),
                     sds((rows, 512), BF16), sds((rows, LANES), BF16), sds((nb, ATT_HEADS, seq), F32),
                     sds((rows, 512), F32), sds((rows, 512), BF16)]
        out_specs = [pl.BlockSpec((512, tm), lambda i: (0, i)), tl(512), tl(512),
                     row(512), row(LANES), tl(ATT_HEADS), row(512), row(512)]
        scratch = [pltpu.VMEM((ATT_HEADS, LANES), F32)]
    else:
        out_shape = [sds((rows, 512), F32)] * 3 + [sds((rows, ATT_HEADS), F32)] + [sds((rows, 512), F32)] * 2
        out_specs = [row(512)] * 3 + [row(ATT_HEADS), row(512), row(512)]
        scratch = []
    return pl.pallas_call(
        functools.partial(_inproj_body, prompt=prompt, tm=tm, tiles_per_batch=tiles_per_batch),
        grid=(rows // tm,), in_specs=in_specs, out_specs=out_specs, out_shape=out_shape,
        scratch_shapes=scratch, compiler_params=_params("arbitrary"),
        name="inproj_prompt" if prompt else "inproj_sample")(*args)


def _fox_body(qt_ref, kb_ref, aug_ref, vt_ref, o_ref, vb_ref, ka_ref, s_ref, p_ref, *, seq, tq):
    nq = seq // tq
    pair = pl.program_id(1)
    vb_ref[...] = vt_ref[0].astype(BF16)
    ka_ref[:, 0:LANES] = kb_ref[...]
    ka_ref[:, LANES:] = aug_ref[...]
    key = lax.broadcasted_iota(jnp.int32, (tq, tq), 0)
    qry = lax.broadcasted_iota(jnp.int32, (tq, tq), 1)
    causal = key <= qry
    sub = lax.broadcasted_iota(jnp.int32, (LANES, tq), 0)
    zero = jnp.zeros((ATT_HEAD_DIM, tq), BF16)
    picks = []
    for hh in range(2):
        h = 2 * pair + hh
        hit = (sub == h) | (sub == h + ATT_HEADS) | (sub == h + 2 * ATT_HEADS)
        picks.append(jnp.where(hit, 1.0, 0.0).astype(BF16))
    causal2 = jnp.concatenate([causal, causal], axis=1)
    colmax = lambda a: jnp.max(a, axis=0, keepdims=True)
    colsum = lambda a: jnp.sum(a, axis=0, keepdims=True)

    def scores(i):
        q0, q1 = i * tq, (i + 1) * tq
        qa = qt_ref[0:ATT_HEAD_DIM, q0:q1]
        qb = qt_ref[ATT_HEAD_DIM:, q0:q1]
        rhs = jnp.concatenate([jnp.concatenate([qa, zero, picks[0]], axis=0),
                               jnp.concatenate([zero, qb, picks[1]], axis=0)], axis=1)
        s_ref[i % 2, 0:q1, :] = _dot(ka_ref[0:q1, :], rhs)

    def finish(i):
        q0, q1 = i * tq, (i + 1) * tq
        sv = s_ref.at[i % 2]
        pv = p_ref.at[i % 2]
        tail = jnp.where(causal2, sv[q0:q1, :], -jnp.inf)
        m = colmax(tail)
        if i > 0:
            m = jnp.maximum(m, colmax(sv[0:q0, :]))
        pt = jnp.exp2(tail - m)
        l = colsum(pt)
        pv[q0:q1, :] = pt.astype(BF16)
        if i > 0:
            ph = jnp.exp2(sv[0:q0, :] - m)
            l = l + colsum(ph)
            pv[0:q0, :] = ph.astype(BF16)
        halves = []
        for hh in range(2):
            r0, r1 = hh * ATT_HEAD_DIM, (hh + 1) * ATT_HEAD_DIM
            c0, c1 = hh * tq, (hh + 1) * tq
            halves.append(_dot(vb_ref[r0:r1, 0:q1], pv[0:q1, c0:c1]) / l[:, c0:c1])
        o_ref[q0:q1, :] = jnp.concatenate(halves, axis=0).T

    scores(0)
    for i in range(nq):
        if i + 1 < nq:
            scores(i + 1)
        finish(i)


def _fox_prompt(qt, kb, aug, vt, *, batch, seq, tq):
    pairs = ATT_HEADS // 2
    return pl.pallas_call(
        functools.partial(_fox_body, seq=seq, tq=tq),
        grid=(batch, pairs),
        in_specs=[pl.BlockSpec((LANES, seq), lambda b, p: (p, b)),
                  pl.BlockSpec((seq, LANES), lambda b, p: (b, p)),
                  pl.BlockSpec((seq, LANES), lambda b, p: (b, 0)),
                  pl.BlockSpec((1, LANES, seq), lambda b, p: (b, p, 0))],
        out_specs=pl.BlockSpec((seq, LANES), lambda b, p: (b, p)),
        out_shape=jax.ShapeDtypeStruct((batch * seq, ATT_WIDTH), F32),
        scratch_shapes=[pltpu.VMEM((LANES, seq), BF16), pltpu.VMEM((seq, 2 * LANES), BF16),
                        pltpu.VMEM((2, seq, 2 * tq), F32), pltpu.VMEM((2, seq, 2 * tq), BF16)],
        compiler_params=_params("arbitrary", "arbitrary"),
        name="fox_prompt")(qt, kb, aug, vt)


def _merge_body(att_ref, u_ref, z_ref, x_ref, ws_ref, bs_ref, ga_ref, gs_ref, wo_ref,
                o_ref, sgu_ref, *, tm):
    rowi = lax.broadcasted_iota(jnp.int32, (CHUNK, CHUNK), 0)
    coli = lax.broadcasted_iota(jnp.int32, (CHUNK, CHUNK), 1)
    tril = coli <= rowi
    wm = [jnp.where(tril, ws_ref[h], 0.0).astype(BF16) for h in range(ATT_HEADS)]
    first = lax.broadcasted_iota(jnp.int32, (CHUNK, LANES), 1) < ATT_HEAD_DIM
    for c in range(tm // CHUNK):
        r0, r1 = c * CHUNK, (c + 1) * CHUNK
        for p in range(SGU_WIDTH // LANES):
            c0, c1 = p * LANES, (p + 1) * LANES
            zp = z_ref[r0:r1, c0:c1]
            mixed = jnp.where(first, _dot(wm[2 * p], zp), _dot(wm[2 * p + 1], zp)) + bs_ref[:, c0:c1]
            sgu_ref[r0:r1, c0:c1] = u_ref[r0:r1, c0:c1] * mixed
    an = _rms(att_ref[...], ga_ref[...]).astype(BF16)
    sn = _rms(sgu_ref[...], gs_ref[...]).astype(BF16)
    o_ref[...] = x_ref[...] + _dot(an, wo_ref[0:ATT_WIDTH, :]) + _dot(sn, wo_ref[ATT_WIDTH:, :])


def _merge_prompt(att, u, zb, x2d, ws, bs_full, ga, gs, wo, *, tm):
    rows = x2d.shape[0]
    row = lambda n: pl.BlockSpec((tm, n), lambda i: (i, 0))
    return pl.pallas_call(
        functools.partial(_merge_body, tm=tm),
        grid=(rows // tm,),
        in_specs=[row(512), row(512), row(512), row(D_MODEL),
                  _const((ATT_HEADS, CHUNK, CHUNK)), _const((CHUNK, SGU_WIDTH)),
                  _const((1, 512)), _const((1, 512)), _const((D_MODEL, D_MODEL))],
        out_specs=row(D_MODEL),
        out_shape=jax.ShapeDtypeStruct((rows, D_MODEL), F32),
        scratch_shapes=[pltpu.VMEM((tm, SGU_WIDTH), F32)],
        compiler_params=_params("arbitrary"), name="merge_prompt")(att, u, zb, x2d, ws, bs_full, ga, gs, wo)


def _memkv_body(m_ref, g_ref, wk_ref, wv_ref, k_ref, v_ref, kb_ref, vb_ref):
    mn = _rms(m_ref[...], g_ref[...]).astype(BF16)
    k = _dot(mn, wk_ref[...])
    v = _dot(mn, wv_ref[...])
    k_ref[...] = k
    v_ref[...] = v
    kb_ref[...] = k.astype(BF16)
    vb_ref[...] = v.astype(BF16)


def _memkv(mem2d, g, wk, wv, *, tm):
    rows = mem2d.shape[0]
    row = pl.BlockSpec((tm, D_MODEL), lambda i: (i, 0))
    sds = jax.ShapeDtypeStruct
    return pl.pallas_call(
        _memkv_body, grid=(rows // tm,),
        in_specs=[row, _const((1, D_MODEL)), _const((D_MODEL, D_MODEL)), _const((D_MODEL, D_MODEL))],
        out_specs=[row] * 4,
        out_shape=[sds((rows, D_MODEL), F32)] * 2 + [sds((rows, D_MODEL), BF16)] * 2,
        compiler_params=_params("arbitrary"), name="memkv")(mem2d, g, wk, wv)


def _cross_body(x_ref, mk_ref, mv_ref, g_ref, wq_ref, wo_ref, o_ref, oc_ref):
    x = x_ref[...]
    xn = _rms(x, g_ref[...]).astype(BF16)
    q = (_dot(xn, wq_ref[...]) * (MEM_HEAD_DIM ** -0.5)).astype(BF16)
    for h in range(MEM_HEADS):
        c0, c1 = h * MEM_HEAD_DIM, (h + 1) * MEM_HEAD_DIM
        s = _dot_nt(q[:, c0:c1], mk_ref[:, c0:c1])
        p = jnp.exp(s - jnp.max(s, axis=1, keepdims=True))
        l = jnp.sum(p, axis=1, keepdims=True)
        oc_ref[:, c0:c1] = (_dot(p.astype(BF16), mv_ref[:, c0:c1]) / l).astype(BF16)
    o_ref[...] = x + _dot(oc_ref[...], wo_ref[...])


def _cross_prompt(x2d, mkb, mvb, g, wq, wo, *, batch, seq, tm):
    nt = seq // tm
    row = pl.BlockSpec((tm, D_MODEL), lambda b, i: (b * nt + i, 0))
    mem = pl.BlockSpec((N_MEM, D_MODEL), lambda b, i: (b, 0))
    return pl.pallas_call(
        _cross_body, grid=(batch, nt),
        in_specs=[row, mem, mem, _const((1, D_MODEL)), _const((D_MODEL, D_MODEL)), _const((D_MODEL, D_MODEL))],
        out_specs=row, out_shape=jax.ShapeDtypeStruct(x2d.shape, F32),
        scratch_shapes=[pltpu.VMEM((tm, D_MODEL), BF16)],
        compiler_params=_params("arbitrary", "arbitrary"), name="cross_prompt")(x2d, mkb, mvb, g, wq, wo)


HALO = 16
FF_TILE = 256


def _ffn_body(x_ref, xp_ref, g_ref, wu_ref, cw_ref, cb_ref, wd_ref, gf_ref,
              y_ref, cg_ref, cv_ref, hg_ref, hv_ref, acc_ref, *, tm, tiles_per_batch):
    first = pl.program_id(0) % tiles_per_batch == 0
    x = x_ref[...]
    g = g_ref[...]
    xn = _rms(x, g).astype(BF16)
    xpn = _rms(xp_ref[...], g).astype(BF16)
    acc_ref[...] = x
    for c in range(D_FF // FF_TILE):
        a, b = c * FF_TILE, (c + 1) * FF_TILE
        halves = []
        for off, h_ref, c_ref in ((0, hg_ref, cg_ref), (D_FF, hv_ref, cv_ref)):
            w = wu_ref[:, off + a:off + b]
            hp = _dot(xpn, w)
            h_ref[0:HALO, :] = jnp.where(first, jnp.zeros_like(hp), hp)
            h_ref[HALO:HALO + tm, :] = _dot(xn, w)
            cw = cw_ref[:, off + a:off + b]
            halves.append(cb_ref[:, off + a:off + b]
                          + cw[0:1] * h_ref[HALO - 2:HALO - 2 + tm, :]
                          + cw[1:2] * h_ref[HALO - 1:HALO - 1 + tm, :]
                          + cw[2:3] * h_ref[HALO:HALO + tm, :])
            c_ref[0, :, a:b] = h_ref[HALO + tm - 2:HALO + tm, :]
        act = (_silu(halves[0]) * halves[1]).astype(BF16)
        acc_ref[...] += _dot(act, wd_ref[a:b, :])
    y_ref[...] = _rms(acc_ref[...], gf_ref[...])


def _ffn_prompt(x2d, g, wu, cw, cb, wd, gf, *, batch, seq, tm):
    rows = x2d.shape[0]
    tpb = seq // tm
    hb = tm // HALO
    single = dict(pipeline_mode=pl.Buffered(1))
    sds = jax.ShapeDtypeStruct
    conv_spec = pl.BlockSpec((1, CONV_W - 1, D_FF), lambda i: (i // tpb, 0, 0))
    return pl.pallas_call(
        functools.partial(_ffn_body, tm=tm, tiles_per_batch=tpb),
        grid=(rows // tm,),
        in_specs=[pl.BlockSpec((tm, D_MODEL), lambda i: (i, 0)),
                  pl.BlockSpec((HALO, D_MODEL), lambda i: (jnp.maximum(i * hb - 1, 0), 0)),
                  _const((1, D_MODEL)),
                  pl.BlockSpec((D_MODEL, 2 * D_FF), lambda i: (0, 0), **single),
                  _const((CONV_W, 2 * D_FF)), _const((1, 2 * D_FF)),
                  pl.BlockSpec((D_FF, D_MODEL), lambda i: (0, 0), **single),
                  _const((1, D_MODEL))],
        out_specs=[pl.BlockSpec((tm, D_MODEL), lambda i: (i, 0)), conv_spec, conv_spec],
        out_shape=[sds((rows, D_MODEL), F32), sds((batch, CONV_W - 1, D_FF), F32),
                   sds((batch, CONV_W - 1, D_FF), F32)],
        scratch_shapes=[pltpu.VMEM((HALO + tm, FF_TILE), F32), pltpu.VMEM((HALO + tm, FF_TILE), F32),
                        pltpu.VMEM((tm, D_MODEL), F32)],
        compiler_params=_params("arbitrary"), name="ffn_prompt")(x2d, x2d, g, wu, cw, cb, wd, gf)


DEC_PAGES_PER_STEP = 8
HD_ROWS = ATT_HEADS * ATT_HEAD_DIM


def _head_rows(x):
    n = x.shape[-1]
    return jnp.broadcast_to(x[:, None, :], (ATT_HEADS, ATT_HEAD_DIM, n)).reshape(HD_ROWS, n)


def _head_sum(x):
    n = x.shape[-1]
    return jnp.sum(x.reshape(ATT_HEADS, ATT_HEAD_DIM, n), axis=1)


def _decode_body(pt_ref, q_ref, kn_ref, vn_ref, lfn_ref, *rest, npg):
    del pt_ref
    k_refs, v_refs, lf_refs = rest[:npg], rest[npg:2 * npg], rest[2 * npg:3 * npg]
    o_ref, m_ref, l_ref, acc_ref, carry_ref = rest[3 * npg:]
    j = pl.program_id(1)
    qcol = q_ref[0] * (ATT_HEAD_DIM ** -0.5)
    lane8 = lax.broadcasted_iota(jnp.int32, (ATT_HEADS, LANES), 1)

    @pl.when(j == 0)
    def _():
        s_new = _head_sum(qcol * kn_ref[0])
        m_ref[...] = jnp.broadcast_to(s_new, m_ref.shape)
        l_ref[...] = jnp.where(lane8 == 0, 1.0, 0.0)
        lane = lax.broadcasted_iota(jnp.int32, (HD_ROWS, LANES), 1)
        acc_ref[...] = jnp.where(lane == 0, jnp.broadcast_to(vn_ref[0], (HD_ROWS, LANES)), 0.0)
        carry_ref[...] = jnp.broadcast_to(lfn_ref[0], carry_ref.shape)

    qrep = jnp.broadcast_to(qcol, (HD_ROWS, LANES))
    carry = carry_ref[...]
    scores = []
    for i in range(npg):
        lf = lf_refs[i][0]
        inc = lf
        for sh in (1, 2, 4, 8, 16, 32, 64):
            inc = inc + jnp.where(lane8 < LANES - sh, pltpu.roll(inc, LANES - sh, 1), 0.0)
        scores.append(_head_sum(k_refs[i][0] * qrep) + (carry + (inc - lf)))
        carry = carry + jnp.broadcast_to(inc[:, 0:1], carry.shape)
    carry_ref[...] = carry

    mx = functools.reduce(jnp.maximum, scores)
    m_old = m_ref[...]
    m_new = jnp.maximum(m_old, jnp.broadcast_to(jnp.max(mx, axis=1, keepdims=True), m_old.shape))
    alpha = jnp.exp(m_old - m_new)
    m_ref[...] = m_new
    probs = [jnp.exp(s - m_new) for s in scores]
    l_ref[...] = alpha * l_ref[...] + functools.reduce(jnp.add, probs)
    acc = acc_ref[...] * _head_rows(alpha)
    for i in range(npg):
        acc = acc + v_refs[i][0] * _head_rows(probs[i])
    acc_ref[...] = acc

    @pl.when(j == pl.num_programs(1) - 1)
    def _():
        l = jnp.sum(l_ref[...], axis=1, keepdims=True)
        o_ref[0] = jnp.sum(acc, axis=1, keepdims=True) / _head_rows(l)


def _fox_decode(page_table, qcol, kncol, vncol, lfncol, kt, vt, lft):
    nb, npages = page_table.shape
    npg = DEC_PAGES_PER_STEP
    col = lambda n: pl.BlockSpec((1, n, 1), lambda b, j, pt: (b, 0, 0))

    def page(rows, i):
        return pl.BlockSpec((1, rows, PAGE_SIZE),
                            lambda b, j, pt: (pt[b, npages - 1 - (j * npg + i)], 0, 0))

    in_specs = ([col(HD_ROWS), col(HD_ROWS), col(HD_ROWS), col(ATT_HEADS)]
                + [page(HD_ROWS, i) for i in range(npg)] * 2
                + [page(ATT_HEADS, i) for i in range(npg)])
    grid_spec = pltpu.PrefetchScalarGridSpec(
        num_scalar_prefetch=1, grid=(nb, npages // npg), in_specs=in_specs,
        out_specs=col(HD_ROWS),
        scratch_shapes=[pltpu.VMEM((ATT_HEADS, LANES), F32), pltpu.VMEM((ATT_HEADS, LANES), F32),
                        pltpu.VMEM((HD_ROWS, LANES), F32), pltpu.VMEM((ATT_HEADS, LANES), F32)])
    return pl.pallas_call(
        functools.partial(_decode_body, npg=npg), grid_spec=grid_spec,
        out_shape=jax.ShapeDtypeStruct((nb, HD_ROWS, 1), F32),
        compiler_params=_params("arbitrary", "arbitrary"), name="fox_decode")(
            page_table, qcol, kncol, vncol, lfncol, *([kt] * npg), *([vt] * npg), *([lft] * npg))


def _merge_sample_body(att_ref, u_ref, z_ref, x_ref, w0_ref, b0_ref, ga_ref, gs_ref, wo_ref,
                       gc_ref, wq_ref, x1_ref, q_ref):
    sgu = u_ref[...] * (w0_ref[...] * z_ref[...] + b0_ref[...])
    an = _rms(att_ref[...], ga_ref[...]).astype(BF16)
    sn = _rms(sgu, gs_ref[...]).astype(BF16)
    x1 = x_ref[...] + _dot(an, wo_ref[0:ATT_WIDTH, :]) + _dot(sn, wo_ref[ATT_WIDTH:, :])
    x1_ref[...] = x1
    q_ref[...] = _dot(_rms(x1, gc_ref[...]).astype(BF16), wq_ref[...]) * (MEM_HEAD_DIM ** -0.5)


def _merge_sample(att, u, z, x2d, w0, b0, ga, gs, wo, gc, wq):
    sds = jax.ShapeDtypeStruct
    return pl.pallas_call(
        _merge_sample_body,
        out_shape=[sds(x2d.shape, F32), sds(x2d.shape, F32)],
        compiler_params=pltpu.CompilerParams(vmem_limit_bytes=VMEM_LIMIT),
        name="merge_sample")(att, u, z, x2d, w0, b0, ga, gs, wo, gc, wq)


def _cross_sample_body(q_ref, mk_ref, mv_ref, o_ref):
    q = q_ref[0]
    rowi = lax.broadcasted_iota(jnp.int32, (8, D_MODEL), 0)
    head = lax.broadcasted_iota(jnp.int32, (8, D_MODEL), 1) // MEM_HEAD_DIM
    own = rowi == head
    qbd = jnp.where(own, jnp.broadcast_to(q, (8, D_MODEL)), 0.0).astype(BF16)
    s = _dot_nt(qbd, mk_ref[0].astype(BF16))
    p = jnp.exp(s - jnp.max(s, axis=1, keepdims=True))
    l = jnp.sum(p, axis=1, keepdims=True)
    o = _dot(p.astype(BF16), mv_ref[0].astype(BF16)) / l
    o_ref[0] = jnp.sum(jnp.where(own, o, 0.0), axis=0, keepdims=True)


def _cross_sample(q3, mk, mv):
    nb = q3.shape[0]
    vec = pl.BlockSpec((1, 1, D_MODEL), lambda b: (b, 0, 0))
    mem = pl.BlockSpec((1, N_MEM, D_MODEL), lambda b: (b, 0, 0))
    return pl.pallas_call(
        _cross_sample_body, grid=(nb,), in_specs=[vec, mem, mem], out_specs=vec,
        out_shape=jax.ShapeDtypeStruct(q3.shape, F32),
        compiler_params=_params("arbitrary"), name="cross_sample")(q3, mk, mv)


def _ffn_sample_body(o_ref, x_ref, wco_ref, g_ref, wug_ref, wuv_ref, cwg_ref, cwv_ref, cbg_ref, cbv_ref,
                     p0g_ref, p0v_ref, p1g_ref, p1v_ref, wd_ref, gf_ref,
                     y_ref, hg_ref, hv_ref, x2_ref, xn_ref):
    j = pl.program_id(0)

    @pl.when(j == 0)
    def _():
        x2 = x_ref[...] + _dot(o_ref[...].astype(BF16), wco_ref[...])
        x2_ref[...] = x2
        xn_ref[...] = _rms(x2, g_ref[...]).astype(BF16)

    xn = xn_ref[...]
    hg = _dot(xn, wug_ref[...])
    hv = _dot(xn, wuv_ref[...])
    hg_ref[...] = hg
    hv_ref[...] = hv
    cwg = cwg_ref[...]
    cwv = cwv_ref[...]
    gate = cbg_ref[...] + cwg[0:1] * p0g_ref[...] + cwg[1:2] * p1g_ref[...] + cwg[2:3] * hg
    val = cbv_ref[...] + cwv[0:1] * p0v_ref[...] + cwv[1:2] * p1v_ref[...] + cwv[2:3] * hv
    x2_ref[...] += _dot((_silu(gate) * val).astype(BF16), wd_ref[...])

    @pl.when(j == pl.num_programs(0) - 1)
    def _():
        y_ref[...] = _rms(x2_ref[...], gf_ref[...])


def _ffn_sample(o, x1, wco, g, wu, cw, cb, prev0, prev1, wd, gf):
    nb = x1.shape[0]
    nf = D_FF // FF_TILE
    gcol = lambda r: pl.BlockSpec((r, FF_TILE), lambda j: (0, j))
    vcol = lambda r: pl.BlockSpec((r, FF_TILE), lambda j: (0, nf + j))
    sds = jax.ShapeDtypeStruct
    return pl.pallas_call(
        _ffn_sample_body, grid=(nf,),
        in_specs=[_const((nb, D_MODEL)), _const((nb, D_MODEL)), _const((D_MODEL, D_MODEL)), _const((1, D_MODEL)),
                  gcol(D_MODEL), vcol(D_MODEL), gcol(CONV_W), vcol(CONV_W), gcol(1), vcol(1),
                  gcol(nb), vcol(nb), gcol(nb), vcol(nb),
                  pl.BlockSpec((FF_TILE, D_MODEL), lambda j: (j, 0)), _const((1, D_MODEL))],
        out_specs=[_const((nb, D_MODEL)), gcol(nb), gcol(nb)],
        out_shape=[sds((nb, D_MODEL), F32), sds((nb, D_FF), F32), sds((nb, D_FF), F32)],
        scratch_shapes=[pltpu.VMEM((nb, D_MODEL), F32), pltpu.VMEM((nb, D_MODEL), BF16)],
        compiler_params=_params("arbitrary"), name="ffn_sample")(
            o, x1, wco, g, wu, wu, cw, cw, cb, cb, prev0, prev0, prev1, prev1, wd, gf)


PROMPT_TILE = 512
ATT_TILE = 256


def kernel(x_prompt, x_sample, cache_k, cache_v, cache_logf, cache_mem_k, cache_mem_v, state_conv,
           page_table, mem_prompt, g_mix, w_in, b_f, g_sgu_v, w_s, b_s, g_att_out, g_sgu_out, w_o,
           g_cross, g_mem, w_cq, w_ck, w_cv, w_co, g_ffn, w_up, conv_w, conv_b, w_down, g_final):
    depth = w_in.shape[0]
    bp, seq, _ = x_prompt.shape
    bs = x_sample.shape[0]
    assert depth == 1 and x_sample.shape[1] == 1 and seq % PROMPT_TILE == 0
    n_phys = cache_k.shape[1]
    tm = PROMPT_TILE
    tpb = seq // tm
    nq = seq // ATT_TILE
    row1 = lambda a: a.reshape(1, -1)

    xp = x_prompt.reshape(bp * seq, D_MODEL)
    xs = x_sample.reshape(bs, D_MODEL)
    tri = (lax.broadcasted_iota(jnp.int32, (tm, tm), 0)
           <= lax.broadcasted_iota(jnp.int32, (tm, tm), 1)).astype(BF16)
    gfin = row1(g_final)
    outs = [[] for _ in range(11)]
    for l in range(depth):
        wl = w_in[l]
        w_r = jnp.concatenate([wl[:, :O_F], wl[:, O_U:],
                               jnp.pad(wl[:, O_F:O_U], ((0, 0), (0, LANES - ATT_HEADS)))], axis=1).astype(BF16)
        bf128 = jnp.pad(b_f[l], (0, LANES - ATT_HEADS)).reshape(1, LANES)
        wo_b, wcq_b, wck_b, wcv_b, wco_b = (w[l].astype(BF16) for w in (w_o, w_cq, w_ck, w_cv, w_co))
        wu_b, wd_b = w_up[l].astype(BF16), w_down[l].astype(BF16)
        gmix, gz, ga, gs = row1(g_mix[l]), row1(g_sgu_v[l]), row1(g_att_out[l]), row1(g_sgu_out[l])
        gc, gm, gff = row1(g_cross[l]), row1(g_mem[l]), row1(g_ffn[l])
        cb = row1(conv_b[l])
        bs_full = jnp.repeat(b_s[l].T, ATT_HEAD_DIM, axis=1)

        qt, kt, vt, kb, aug, lft_p, u, zb = _inproj(xp, gmix, w_r, bf128, gz, tri, prompt=True,
                                                   tm=tm, tiles_per_batch=tpb)
        att = _fox_prompt(qt, kb, aug, vt, batch=bp, seq=seq, tq=ATT_TILE)
        x1 = _merge_prompt(att, u, zb, xp, w_s[l], bs_full, ga, gs, wo_b, tm=tm)
        mk, mv, mkb, mvb = _memkv(mem_prompt.reshape(bp * N_MEM, D_MODEL), gm, wck_b, wcv_b, tm=tm)
        x2 = _cross_prompt(x1, mkb, mvb, gc, wcq_b, wco_b, batch=bp, seq=seq, tm=tm)
        xp, cg, cv = _ffn_prompt(x2, gff, wu_b, conv_w[l], cb, wd_b, gfin, batch=bp, seq=seq, tm=tm)
        heads_t = lambda a: jnp.transpose(a.reshape(bp, ATT_HEADS, ATT_HEAD_DIM, seq), (0, 3, 1, 2))
        outs[0].append(heads_t(kt))
        outs[1].append(heads_t(vt))
        outs[2].append(jnp.transpose(lft_p, (0, 2, 1)))
        outs[3].append(mk.reshape(bp, N_MEM, MEM_HEADS, MEM_HEAD_DIM))
        outs[4].append(mv.reshape(bp, N_MEM, MEM_HEADS, MEM_HEAD_DIM))
        outs[5].append(jnp.concatenate([cg, cv], axis=-1))

        qs, ks, vs, lfs, us, zs = _inproj(xs, gmix, w_r, bf128, gz, None, prompt=False,
                                          tm=bs, tiles_per_batch=1)
        kt = jnp.transpose(cache_k[l], (0, 2, 3, 1)).reshape(n_phys, HD_ROWS, PAGE_SIZE)
        vt = jnp.transpose(cache_v[l], (0, 2, 3, 1)).reshape(n_phys, HD_ROWS, PAGE_SIZE)
        lft = jnp.transpose(cache_logf[l], (0, 2, 1))
        col = lambda a: a.reshape(bs, -1, 1)
        att_s = _fox_decode(page_table, col(qs), col(ks), col(vs), col(lfs), kt, vt, lft).reshape(bs, ATT_WIDTH)
        w0 = row1(jnp.repeat(w_s[l][:, 0, 0], ATT_HEAD_DIM))
        b0 = row1(jnp.repeat(b_s[l][:, 0], ATT_HEAD_DIM))
        x1s, qcs = _merge_sample(att_s, us, zs, xs, w0, b0, ga, gs, wo_b, gc, wcq_b)
        oc = _cross_sample(qcs.reshape(bs, 1, D_MODEL),
                           cache_mem_k[l].reshape(bs, N_MEM, D_MODEL),
                           cache_mem_v[l].reshape(bs, N_MEM, D_MODEL)).reshape(bs, D_MODEL)
        prev0, prev1 = state_conv[l][:, 0, :], state_conv[l][:, 1, :]
        xs, hg, hv = _ffn_sample(oc, x1s, wco_b, gff, wu_b, conv_w[l], cb, prev0, prev1, wd_b, gfin)
        h_new = jnp.concatenate([hg, hv], axis=-1)
        outs[6].append(ks.reshape(bs, 1, ATT_HEADS, ATT_HEAD_DIM))
        outs[7].append(vs.reshape(bs, 1, ATT_HEADS, ATT_HEAD_DIM))
        outs[8].append(lfs.reshape(bs, 1, ATT_HEADS))
        outs[9].append(zs.reshape(bs, 1, ATT_HEADS, ATT_HEAD_DIM))
        outs[10].append(jnp.stack([prev1, h_new], axis=1))

    y_prompt = xp.reshape(bp, seq, D_MODEL)
    y_sample = xs.reshape(bs, 1, D_MODEL)
    return (y_prompt, y_sample) + tuple(jnp.stack(o) for o in outs)
```

```python
import functools
import math

import jax
import jax.numpy as jnp
from jax import lax
from jax.experimental import pallas as pl
from jax.experimental.pallas import tpu as pltpu

D_MODEL = 1024
PAGE_SIZE = 128
CHUNK = 128
ATT_HEADS = 8
ATT_HEAD_DIM = 64
ATT_WIDTH = ATT_HEADS * ATT_HEAD_DIM
SGU_WIDTH = 512
O_F = 3 * ATT_WIDTH
O_U = O_F + ATT_HEADS
N_MEM = 256
MEM_HEADS = 4
MEM_HEAD_DIM = D_MODEL // MEM_HEADS
D_FF = 2816
CONV_W = 3
EPS = 1e-6

LOG2E = math.log2(math.e)
LANES = 128
W_IN_COLS = 5 * 512 + LANES
VMEM_LIMIT = 56 * 1024 * 1024

F32 = jnp.float32
BF16 = jnp.bfloat16


def _rms(x, g):
    return x * lax.rsqrt(jnp.mean(x * x, axis=-1, keepdims=True) + EPS) * g


def _gelu(x):
    c = math.sqrt(2.0 / math.pi)
    return x * (0.5 * (1.0 + jnp.tanh(c * (x + 0.044715 * (x * x * x)))))


def _log_sigmoid(x):
    return jnp.minimum(x, 0.0) - jnp.log1p(jnp.exp(-jnp.abs(x)))


def _silu(x):
    return x * (0.5 * (1.0 + jnp.tanh(0.5 * x)))


def _dot(a, b):
    return jnp.dot(a, b, preferred_element_type=F32)


def _dot_nt(a, b):
    return lax.dot_general(a, b, (((1,), (1,)), ((), ())), preferred_element_type=F32)


def _split3(x):
    hi = x.astype(BF16)
    r = x - hi.astype(F32)
    mid = r.astype(BF16)
    lo = (r - mid.astype(F32)).astype(BF16)
    return hi, mid, lo


def _params(*sem):
    return pltpu.CompilerParams(dimension_semantics=sem, vmem_limit_bytes=VMEM_LIMIT)


def _const(shape):
    return pl.BlockSpec(shape, lambda *_: (0,) * len(shape))


def _inproj_body(*refs, prompt, tm, tiles_per_batch):
    if prompt:
        (x_ref, g_ref, w_ref, bf_ref, gz_ref, tri_ref,
         qt_ref, kt_ref, vt_ref, kb_ref, aug_ref, lft_ref, u_ref, zb_ref, carry_ref) = refs
    else:
        (x_ref, g_ref, w_ref, bf_ref, gz_ref,
         q_ref, k_ref, v_ref, lf_ref, u_ref, z_ref) = refs
    xn = _rms(x_ref[...], g_ref[...]).astype(BF16)

    def proj(a, b):
        return _dot(xn, w_ref[:, a:b])

    q = proj(0, 512)
    k = proj(512, 1024)
    v = proj(1024, 1536)
    u_ref[...] = _gelu(proj(1536, 2048))
    z = _rms(_gelu(proj(2048, 2560)), gz_ref[...])
    lf = _log_sigmoid(proj(2560, W_IN_COLS) + bf_ref[...])
    if not prompt:
        q_ref[...] = q
        k_ref[...] = k
        v_ref[...] = v
        lf_ref[...] = lf[:, :ATT_HEADS]
        z_ref[...] = z
        return
    qt_ref[...] = (q * (ATT_HEAD_DIM ** -0.5 * LOG2E)).T.astype(BF16)
    kt_ref[0] = k.T
    vt_ref[0] = v.T
    kb_ref[...] = k.astype(BF16)
    zb_ref[...] = z.astype(BF16)

    @pl.when(pl.program_id(0) % tiles_per_batch == 0)
    def _():
        carry_ref[...] = jnp.zeros_like(carry_ref)

    lft = lf.T[:ATT_HEADS, :]
    lft_ref[0] = lft
    tri = tri_ref[...]
    hi, mid, lo = _split3(lft)
    c = _dot(hi, tri) + _dot(mid, tri) + _dot(lo, tri) + carry_ref[:, 0:1]
    carry_ref[...] = jnp.broadcast_to(c[:, tm - 1:tm], carry_ref.shape)
    terms = [t.astype(F32) for t in _split3(c * (-LOG2E))]
    pad = jnp.zeros((LANES - 3 * ATT_HEADS, tm), F32)
    aug_ref[...] = jnp.concatenate(terms + [pad], axis=0).T.astype(BF16)


def _inproj(x2d, g, w_r, bf128, gz, tri, *, prompt, tm, tiles_per_batch):
    rows = x2d.shape[0]
    row = lambda n: pl.BlockSpec((tm, n), lambda i: (i, 0))
    in_specs = [row(D_MODEL), _const((1, D_MODEL)), _const((D_MODEL, W_IN_COLS)),
                _const((1, LANES)), _const((1, SGU_WIDTH))]
    args = [x2d, g, w_r, bf128, gz]
    sds = jax.ShapeDtypeStruct
    if prompt:
        in_specs.append(_const((tm, tm)))
        args.append(tri)
        nb = rows // (tm * tiles_per_batch)
        seq = tm * tiles_per_batch
        tpb = tiles_per_batch
        tl = lambda n: pl.BlockSpec((1, n, tm), lambda i: (i // tpb, 0, i % tpb))
        out_shape = [sds((512, rows), BF16), sds((nb, 512, seq), F32), sds((nb, 512, seq), F32),
                     sds((rows, 512), BF16), sds((rows, LANES), BF16), sds((nb, ATT_HEADS, seq), F32),
                     sds((rows, 512), F32), sds((rows, 512), BF16)]
        out_specs = [pl.BlockSpec((512, tm), lambda i: (0, i)), tl(512), tl(512),
                     row(512), row(LANES), tl(ATT_HEADS), row(512), row(512)]
        scratch = [pltpu.VMEM((ATT_HEADS, LANES), F32)]
    else:
        out_shape = [sds((rows, 512), F32)] * 3 + [sds((rows, ATT_HEADS), F32)] + [sds((rows, 512), F32)] * 2
        out_specs = [row(512)] * 3 + [row(ATT_HEADS), row(512), row(512)]
        scratch = []
    return pl.pallas_call(
        functools.partial(_inproj_body, prompt=prompt, tm=tm, tiles_per_batch=tiles_per_batch),
        grid=(rows // tm,), in_specs=in_specs, out_specs=out_specs, out_shape=out_shape,
        scratch_shapes=scratch, compiler_params=_params("arbitrary"),
        name="inproj_prompt" if prompt else "inproj_sample")(*args)


def _fox_body(qt_ref, kb_ref, aug_ref, vt_ref, o_ref, vb_ref, ka_ref, s_ref, p_ref, *, seq, tq):
    nq = seq // tq
    pair = pl.program_id(1)
    vb_ref[...] = vt_ref[0].astype(BF16)
    ka_ref[:, 0:LANES] = kb_ref[...]
    ka_ref[:, LANES:] = aug_ref[...]
    key = lax.broadcasted_iota(jnp.int32, (tq, tq), 0)
    qry = lax.broadcasted_iota(jnp.int32, (tq, tq), 1)
    causal = key <= qry
    sub = lax.broadcasted_iota(jnp.int32, (LANES, tq), 0)
    zero = jnp.zeros((ATT_HEAD_DIM, tq), BF16)
    picks = []
    for hh in range(2):
        h = 2 * pair + hh
        hit = (sub == h) | (sub == h + ATT_HEADS) | (sub == h + 2 * ATT_HEADS)
        picks.append(jnp.where(hit, 1.0, 0.0).astype(BF16))
    causal2 = jnp.concatenate([causal, causal], axis=1)
    colmax = lambda a: jnp.max(a, axis=0, keepdims=True)
    colsum = lambda a: jnp.sum(a, axis=0, keepdims=True)

    def scores(i):
        q0, q1 = i * tq, (i + 1) * tq
        qa = qt_ref[0:ATT_HEAD_DIM, q0:q1]
        qb = qt_ref[ATT_HEAD_DIM:, q0:q1]
        rhs = jnp.concatenate([jnp.concatenate([qa, zero, picks[0]], axis=0),
                               jnp.concatenate([zero, qb, picks[1]], axis=0)], axis=1)
        s_ref[i % 2, 0:q1, :] = _dot(ka_ref[0:q1, :], rhs)

    def finish(i):
        q0, q1 = i * tq, (i + 1) * tq
        sv = s_ref.at[i % 2]
        pv = p_ref.at[i % 2]
        tail = jnp.where(causal2, sv[q0:q1, :], -jnp.inf)
        m = colmax(tail)
        if i > 0:
            m = jnp.maximum(m, colmax(sv[0:q0, :]))
        pt = jnp.exp2(tail - m)
        l = colsum(pt)
        pv[q0:q1, :] = pt.astype(BF16)
        if i > 0:
            ph = jnp.exp2(sv[0:q0, :] - m)
            l = l + colsum(ph)
            pv[0:q0, :] = ph.astype(BF16)
        halves = []
        for hh in range(2):
            r0, r1 = hh * ATT_HEAD_DIM, (hh + 1) * ATT_HEAD_DIM
            c0, c1 = hh * tq, (hh + 1) * tq
            halves.append(_dot(vb_ref[r0:r1, 0:q1], pv[0:q1, c0:c1]) / l[:, c0:c1])
        o_ref[q0:q1, :] = jnp.concatenate(halves, axis=0).T

    scores(0)
    for i in range(nq):
        if i + 1 < nq:
            scores(i + 1)
        finish(i)


def _fox_prompt(qt, kb, aug, vt, *, batch, seq, tq):
    pairs = ATT_HEADS // 2
    return pl.pallas_call(
        functools.partial(_fox_body, seq=seq, tq=tq),
        grid=(batch, pairs),
        in_specs=[pl.BlockSpec((LANES, seq), lambda b, p: (p, b)),
                  pl.BlockSpec((seq, LANES), lambda b, p: (b, p)),
                  pl.BlockSpec((seq, LANES), lambda b, p: (b, 0)),
                  pl.BlockSpec((1, LANES, seq), lambda b, p: (b, p, 0))],
        out_specs=pl.BlockSpec((seq, LANES), lambda b, p: (b, p)),
        out_shape=jax.ShapeDtypeStruct((batch * seq, ATT_WIDTH), F32),
        scratch_shapes=[pltpu.VMEM((LANES, seq), BF16), pltpu.VMEM((seq, 2 * LANES), BF16),
                        pltpu.VMEM((2, seq, 2 * tq), F32), pltpu.VMEM((2, seq, 2 * tq), BF16)],
        compiler_params=_params("arbitrary", "arbitrary"),
        name="fox_prompt")(qt, kb, aug, vt)


def _merge_body(att_ref, u_ref, z_ref, x_ref, ws_ref, bs_ref, ga_ref, gs_ref, wo_ref,
                o_ref, sgu_ref, *, tm):
    rowi = lax.broadcasted_iota(jnp.int32, (CHUNK, CHUNK), 0)
    coli = lax.broadcasted_iota(jnp.int32, (CHUNK, CHUNK), 1)
    tril = coli <= rowi
    wm = [jnp.where(tril, ws_ref[h], 0.0).astype(BF16) for h in range(ATT_HEADS)]
    first = lax.broadcasted_iota(jnp.int32, (CHUNK, LANES), 1) < ATT_HEAD_DIM
    for c in range(tm // CHUNK):
        r0, r1 = c * CHUNK, (c + 1) * CHUNK
        for p in range(SGU_WIDTH // LANES):
            c0, c1 = p * LANES, (p + 1) * LANES
            zp = z_ref[r0:r1, c0:c1]
            mixed = jnp.where(first, _dot(wm[2 * p], zp), _dot(wm[2 * p + 1], zp)) + bs_ref[:, c0:c1]
            sgu_ref[r0:r1, c0:c1] = u_ref[r0:r1, c0:c1] * mixed
    an = _rms(att_ref[...], ga_ref[...]).astype(BF16)
    sn = _rms(sgu_ref[...], gs_ref[...]).astype(BF16)
    o_ref[...] = x_ref[...] + _dot(an, wo_ref[0:ATT_WIDTH, :]) + _dot(sn, wo_ref[ATT_WIDTH:, :])


def _merge_prompt(att, u, zb, x2d, ws, bs_full, ga, gs, wo, *, tm):
    rows = x2d.shape[0]
    row = lambda n: pl.BlockSpec((tm, n), lambda i: (i, 0))
    return pl.pallas_call(
        functools.partial(_merge_body, tm=tm),
        grid=(rows // tm,),
        in_specs=[row(512), row(512), row(512), row(D_MODEL),
                  _const((ATT_HEADS, CHUNK, CHUNK)), _const((CHUNK, SGU_WIDTH)),
                  _const((1, 512)), _const((1, 512)), _const((D_MODEL, D_MODEL))],
        out_specs=row(D_MODEL),
        out_shape=jax.ShapeDtypeStruct((rows, D_MODEL), F32),
        scratch_shapes=[pltpu.VMEM((tm, SGU_WIDTH), F32)],
        compiler_params=_params("arbitrary"), name="merge_prompt")(att, u, zb, x2d, ws, bs_full, ga, gs, wo)


def _memkv_body(m_ref, g_ref, wk_ref, wv_ref, k_ref, v_ref, kb_ref, vb_ref):
    mn = _rms(m_ref[...], g_ref[...]).astype(BF16)
    k = _dot(mn, wk_ref[...])
    v = _dot(mn, wv_ref[...])
    k_ref[...] = k
    v_ref[...] = v
    kb_ref[...] = k.astype(BF16)
    vb_ref[...] = v.astype(BF16)


def _memkv(mem2d, g, wk, wv, *, tm):
    rows = mem2d.shape[0]
    row = pl.BlockSpec((tm, D_MODEL), lambda i: (i, 0))
    sds = jax.ShapeDtypeStruct
    return pl.pallas_call(
        _memkv_body, grid=(rows // tm,),
        in_specs=[row, _const((1, D_MODEL)), _const((D_MODEL, D_MODEL)), _const((D_MODEL, D_MODEL))],
        out_specs=[row] * 4,
        out_shape=[sds((rows, D_MODEL), F32)] * 2 + [sds((rows, D_MODEL), BF16)] * 2,
        compiler_params=_params("arbitrary"), name="memkv")(mem2d, g, wk, wv)


def _cross_body(x_ref, mk_ref, mv_ref, g_ref, wq_ref, wo_ref, o_ref, oc_ref):
    x = x_ref[...]
    xn = _rms(x, g_ref[...]).astype(BF16)
    q = (_dot(xn, wq_ref[...]) * (MEM_HEAD_DIM ** -0.5)).astype(BF16)
    for h in range(MEM_HEADS):
        c0, c1 = h * MEM_HEAD_DIM, (h + 1) * MEM_HEAD_DIM
        s = _dot_nt(q[:, c0:c1], mk_ref[:, c0:c1])
        p = jnp.exp(s - jnp.max(s, axis=1, keepdims=True))
        l = jnp.sum(p, axis=1, keepdims=True)
        oc_ref[:, c0:c1] = (_dot(p.astype(BF16), mv_ref[:, c0:c1]) / l).astype(BF16)
    o_ref[...] = x + _dot(oc_ref[...], wo_ref[...])


def _cross_prompt(x2d, mkb, mvb, g, wq, wo, *, batch, seq, tm):
    nt = seq // tm
    row = pl.BlockSpec((tm, D_MODEL), lambda b, i: (b * nt + i, 0))
    mem = pl.BlockSpec((N_MEM, D_MODEL), lambda b, i: (b, 0))
    return pl.pallas_call(
        _cross_body, grid=(batch, nt),
        in_specs=[row, mem, mem, _const((1, D_MODEL)), _const((D_MODEL, D_MODEL)), _const((D_MODEL, D_MODEL))],
        out_specs=row, out_shape=jax.ShapeDtypeStruct(x2d.shape, F32),
        scratch_shapes=[pltpu.VMEM((tm, D_MODEL), BF16)],
        compiler_params=_params("arbitrary", "arbitrary"), name="cross_prompt")(x2d, mkb, mvb, g, wq, wo)


HALO = 16
FF_TILE = 256


def _ffn_body(x_ref, xp_ref, g_ref, wu_ref, cw_ref, cb_ref, wd_ref, gf_ref,
              y_ref, cg_ref, cv_ref, hg_ref, hv_ref, acc_ref, xn_ref, *, tm, tiles_per_batch):
    first = pl.program_id(0) % tiles_per_batch == 0
    x = x_ref[...]
    g = g_ref[...]
    xn_ref[0:HALO, :] = _rms(xp_ref[...], g).astype(BF16)
    xn_ref[HALO:, :] = _rms(x, g).astype(BF16)
    acc_ref[...] = x
    nchunk = D_FF // FF_TILE
    sides = ((0, hg_ref, cg_ref), (D_FF, hv_ref, cv_ref))

    def up(c):
        a, b = c * FF_TILE, (c + 1) * FF_TILE
        for off, h_ref, _ in sides:
            w = wu_ref[:, off + a:off + b]
            hp = _dot(xn_ref[0:HALO, :], w)
            h_ref[c % 2, 0:HALO, :] = jnp.where(first, jnp.zeros_like(hp), hp)
            h_ref[c % 2, HALO:, :] = _dot(xn_ref[HALO:, :], w)

    def down(c):
        a, b = c * FF_TILE, (c + 1) * FF_TILE
        halves = []
        for off, h_ref, c_ref in sides:
            h = h_ref.at[c % 2]
            cw = cw_ref[:, off + a:off + b]
            halves.append(cb_ref[:, off + a:off + b]
                          + cw[0:1] * h[HALO - 2:HALO - 2 + tm, :]
                          + cw[1:2] * h[HALO - 1:HALO - 1 + tm, :]
                          + cw[2:3] * h[HALO:HALO + tm, :])
            c_ref[0, :, a:b] = h[HALO + tm - 2:HALO + tm, :]
        act = (_silu(halves[0]) * halves[1]).astype(BF16)
        acc_ref[...] += _dot(act, wd_ref[a:b, :])

    up(0)
    for c in range(nchunk):
        if c + 1 < nchunk:
            up(c + 1)
        down(c)
    y_ref[...] = _rms(acc_ref[...], gf_ref[...])


def _ffn_prompt(x2d, g, wu, cw, cb, wd, gf, *, batch, seq, tm):
    rows = x2d.shape[0]
    tpb = seq // tm
    hb = tm // HALO
    single = dict(pipeline_mode=pl.Buffered(1))
    sds = jax.ShapeDtypeStruct
    conv_spec = pl.BlockSpec((1, CONV_W - 1, D_FF), lambda i: (i // tpb, 0, 0))
    return pl.pallas_call(
        functools.partial(_ffn_body, tm=tm, tiles_per_batch=tpb),
        grid=(rows // tm,),
        in_specs=[pl.BlockSpec((tm, D_MODEL), lambda i: (i, 0)),
                  pl.BlockSpec((HALO, D_MODEL), lambda i: (jnp.maximum(i * hb - 1, 0), 0)),
                  _const((1, D_MODEL)),
                  pl.BlockSpec((D_MODEL, 2 * D_FF), lambda i: (0, 0), **single),
                  _const((CONV_W, 2 * D_FF)), _const((1, 2 * D_FF)),
                  pl.BlockSpec((D_FF, D_MODEL), lambda i: (0, 0), **single),
                  _const((1, D_MODEL))],
        out_specs=[pl.BlockSpec((tm, D_MODEL), lambda i: (i, 0)), conv_spec, conv_spec],
        out_shape=[sds((rows, D_MODEL), F32), sds((batch, CONV_W - 1, D_FF), F32),
                   sds((batch, CONV_W - 1, D_FF), F32)],
        scratch_shapes=[pltpu.VMEM((2, HALO + tm, FF_TILE), F32), pltpu.VMEM((2, HALO + tm, FF_TILE), F32),
                        pltpu.VMEM((tm, D_MODEL), F32), pltpu.VMEM((HALO + tm, D_MODEL), BF16)],
        compiler_params=_params("arbitrary"), name="ffn_prompt")(x2d, x2d, g, wu, cw, cb, wd, gf)


DEC_PAGES_PER_STEP = 8
HD_ROWS = ATT_HEADS * ATT_HEAD_DIM


def _head_rows(x):
    n = x.shape[-1]
    return jnp.broadcast_to(x[:, None, :], (ATT_HEADS, ATT_HEAD_DIM, n)).reshape(HD_ROWS, n)


def _head_sum(x):
    n = x.shape[-1]
    return jnp.sum(x.reshape(ATT_HEADS, ATT_HEAD_DIM, n), axis=1)


def _decode_body(pt_ref, q_ref, kn_ref, vn_ref, lfn_ref, u2_ref, *rest, npg, steps):
    del pt_ref
    k_refs, v_refs, lf_refs = rest[:npg], rest[npg:2 * npg], rest[2 * npg:3 * npg]
    o_ref, m_ref, l_ref, acc_ref, carry_ref = rest[3 * npg:]
    j = pl.program_id(0) % steps
    qcol = q_ref[0] * (ATT_HEAD_DIM ** -0.5)
    lane8 = lax.broadcasted_iota(jnp.int32, (ATT_HEADS, LANES), 1)

    @pl.when(j == 0)
    def _():
        s_new = _head_sum(qcol * kn_ref[0])
        m_ref[...] = jnp.broadcast_to(s_new, m_ref.shape)
        l_ref[...] = jnp.where(lane8 == 0, 1.0, 0.0)
        lane = lax.broadcasted_iota(jnp.int32, (HD_ROWS, LANES), 1)
        acc_ref[...] = jnp.where(lane == 0, jnp.broadcast_to(vn_ref[0], (HD_ROWS, LANES)), 0.0)
        carry_ref[...] = jnp.broadcast_to(lfn_ref[0], carry_ref.shape)

    qrep = jnp.broadcast_to(qcol, (HD_ROWS, LANES))
    lfs = [lf_refs[i][0] for i in range(npg)]
    terms = [t.astype(F32) for lf in lfs for t in _split3(lf)]
    scan = _dot(jnp.concatenate(terms, axis=0).astype(BF16), u2_ref[...])
    carry = carry_ref[...]
    scores = []
    for i in range(npg):
        r = 3 * ATT_HEADS * i
        both = scan[r:r + 8] + scan[r + 8:r + 16] + scan[r + 16:r + 24]
        inc, total = both[:, :LANES], both[:, LANES:]
        scores.append(_head_sum(k_refs[i][0] * qrep) + (carry + (inc - lfs[i])))
        carry = carry + total
    carry_ref[...] = carry

    mx = functools.reduce(jnp.maximum, scores)
    m_old = m_ref[...]
    m_new = jnp.maximum(m_old, jnp.broadcast_to(jnp.max(mx, axis=1, keepdims=True), m_old.shape))
    alpha = jnp.exp(m_old - m_new)
    m_ref[...] = m_new
    probs = [jnp.exp(s - m_new) for s in scores]
    l_ref[...] = alpha * l_ref[...] + functools.reduce(jnp.add, probs)
    acc = acc_ref[...] * _head_rows(alpha)
    for i in range(npg):
        acc = acc + v_refs[i][0] * _head_rows(probs[i])
    acc_ref[...] = acc

    @pl.when(j == steps - 1)
    def _():
        l = jnp.sum(l_ref[...], axis=1, keepdims=True)
        o_ref[0] = jnp.sum(acc, axis=1, keepdims=True) / _head_rows(l)


def _fox_decode(page_table, qcol, kncol, vncol, lfncol, kt, vt, lft):
    nb, npages = page_table.shape
    npg = DEC_PAGES_PER_STEP
    steps = npages // npg
    order = page_table[:, ::-1].reshape(-1)
    tok = lax.broadcasted_iota(jnp.int32, (PAGE_SIZE, PAGE_SIZE), 0)
    u2 = jnp.concatenate([(tok >= tok.T).astype(BF16), jnp.ones((PAGE_SIZE, PAGE_SIZE), BF16)], axis=1)
    col = lambda n: pl.BlockSpec((1, n, 1), lambda g, pt: (g // steps, 0, 0))

    def page(rows, i):
        return pl.BlockSpec((1, rows, PAGE_SIZE), lambda g, pt: (pt[g * npg + i], 0, 0))

    in_specs = ([col(HD_ROWS), col(HD_ROWS), col(HD_ROWS), col(ATT_HEADS), _const((PAGE_SIZE, 2 * PAGE_SIZE))]
                + [page(HD_ROWS, i) for i in range(npg)] * 2
                + [page(ATT_HEADS, i) for i in range(npg)])
    grid_spec = pltpu.PrefetchScalarGridSpec(
        num_scalar_prefetch=1, grid=(nb * steps,), in_specs=in_specs,
        out_specs=col(HD_ROWS),
        scratch_shapes=[pltpu.VMEM((ATT_HEADS, LANES), F32), pltpu.VMEM((ATT_HEADS, LANES), F32),
                        pltpu.VMEM((HD_ROWS, LANES), F32), pltpu.VMEM((ATT_HEADS, LANES), F32)])
    return pl.pallas_call(
        functools.partial(_decode_body, npg=npg, steps=steps), grid_spec=grid_spec,
        out_shape=jax.ShapeDtypeStruct((nb, HD_ROWS, 1), F32),
        compiler_params=_params("arbitrary"), name="fox_decode")(
            order, qcol, kncol, vncol, lfncol, u2, *([kt] * npg), *([vt] * npg), *([lft] * npg))


def _merge_sample_body(att_ref, u_ref, z_ref, x_ref, w0_ref, b0_ref, ga_ref, gs_ref, wo_ref,
                       gc_ref, wq_ref, x1_ref, q_ref):
    sgu = u_ref[...] * (w0_ref[...] * z_ref[...] + b0_ref[...])
    an = _rms(att_ref[...], ga_ref[...]).astype(BF16)
    sn = _rms(sgu, gs_ref[...]).astype(BF16)
    x1 = x_ref[...] + _dot(an, wo_ref[0:ATT_WIDTH, :]) + _dot(sn, wo_ref[ATT_WIDTH:, :])
    x1_ref[...] = x1
    q_ref[...] = _dot(_rms(x1, gc_ref[...]).astype(BF16), wq_ref[...]) * (MEM_HEAD_DIM ** -0.5)


def _merge_sample(att, u, z, x2d, w0, b0, ga, gs, wo, gc, wq):
    sds = jax.ShapeDtypeStruct
    return pl.pallas_call(
        _merge_sample_body,
        out_shape=[sds(x2d.shape, F32), sds(x2d.shape, F32)],
        compiler_params=pltpu.CompilerParams(vmem_limit_bytes=VMEM_LIMIT),
        name="merge_sample")(att, u, z, x2d, w0, b0, ga, gs, wo, gc, wq)


def _cross_sample_body(q_ref, mk_ref, mv_ref, o_ref):
    q = q_ref[0]
    rowi = lax.broadcasted_iota(jnp.int32, (8, D_MODEL), 0)
    head = lax.broadcasted_iota(jnp.int32, (8, D_MODEL), 1) // MEM_HEAD_DIM
    own = rowi == head
    qbd = jnp.where(own, jnp.broadcast_to(q, (8, D_MODEL)), 0.0).astype(BF16)
    s = _dot_nt(qbd, mk_ref[0].astype(BF16))
    p = jnp.exp(s - jnp.max(s, axis=1, keepdims=True))
    l = jnp.sum(p, axis=1, keepdims=True)
    o = _dot(p.astype(BF16), mv_ref[0].astype(BF16)) / l
    o_ref[0] = jnp.sum(jnp.where(own, o, 0.0), axis=0, keepdims=True)


def _cross_sample(q3, mk, mv):
    nb = q3.shape[0]
    vec = pl.BlockSpec((1, 1, D_MODEL), lambda b: (b, 0, 0))
    mem = pl.BlockSpec((1, N_MEM, D_MODEL), lambda b: (b, 0, 0))
    return pl.pallas_call(
        _cross_sample_body, grid=(nb,), in_specs=[vec, mem, mem], out_specs=vec,
        out_shape=jax.ShapeDtypeStruct(q3.shape, F32),
        compiler_params=_params("arbitrary"), name="cross_sample")(q3, mk, mv)


def _ffn_sample_body(o_ref, x_ref, wco_ref, g_ref, wug_ref, wuv_ref, cwg_ref, cwv_ref, cbg_ref, cbv_ref,
                     p0g_ref, p0v_ref, p1g_ref, p1v_ref, wd_ref, gf_ref,
                     y_ref, hg_ref, hv_ref, x2_ref, xn_ref):
    j = pl.program_id(0)

    @pl.when(j == 0)
    def _():
        x2 = x_ref[...] + _dot(o_ref[...].astype(BF16), wco_ref[...])
        x2_ref[...] = x2
        xn_ref[...] = _rms(x2, g_ref[...]).astype(BF16)

    xn = xn_ref[...]
    hg = _dot(xn, wug_ref[...])
    hv = _dot(xn, wuv_ref[...])
    hg_ref[...] = hg
    hv_ref[...] = hv
    cwg = cwg_ref[...]
    cwv = cwv_ref[...]
    gate = cbg_ref[...] + cwg[0:1] * p0g_ref[...] + cwg[1:2] * p1g_ref[...] + cwg[2:3] * hg
    val = cbv_ref[...] + cwv[0:1] * p0v_ref[...] + cwv[1:2] * p1v_ref[...] + cwv[2:3] * hv
    x2_ref[...] += _dot((_silu(gate) * val).astype(BF16), wd_ref[...])

    @pl.when(j == pl.num_programs(0) - 1)
    def _():
        y_ref[...] = _rms(x2_ref[...], gf_ref[...])


def _ffn_sample(o, x1, wco, g, wu, cw, cb, prev0, prev1, wd, gf):
    nb = x1.shape[0]
    nf = D_FF // FF_TILE
    gcol = lambda r: pl.BlockSpec((r, FF_TILE), lambda j: (0, j))
    vcol = lambda r: pl.BlockSpec((r, FF_TILE), lambda j: (0, nf + j))
    sds = jax.ShapeDtypeStruct
    return pl.pallas_call(
        _ffn_sample_body, grid=(nf,),
        in_specs=[_const((nb, D_MODEL)), _const((nb, D_MODEL)), _const((D_MODEL, D_MODEL)), _const((1, D_MODEL)),
                  gcol(D_MODEL), vcol(D_MODEL), gcol(CONV_W), vcol(CONV_W), gcol(1), vcol(1),
                  gcol(nb), vcol(nb), gcol(nb), vcol(nb),
                  pl.BlockSpec((FF_TILE, D_MODEL), lambda j: (j, 0)), _const((1, D_MODEL))],
        out_specs=[_const((nb, D_MODEL)), gcol(nb), gcol(nb)],
        out_shape=[sds((nb, D_MODEL), F32), sds((nb, D_FF), F32), sds((nb, D_FF), F32)],
        scratch_shapes=[pltpu.VMEM((nb, D_MODEL), F32), pltpu.VMEM((nb, D_MODEL), BF16)],
        compiler_params=_params("arbitrary"), name="ffn_sample")(
            o, x1, wco, g, wu, wu, cw, cw, cb, cb, prev0, prev0, prev1, prev1, wd, gf)


PROMPT_TILE = 512
ATT_TILE = 256


def kernel(x_prompt, x_sample, cache_k, cache_v, cache_logf, cache_mem_k, cache_mem_v, state_conv,
           page_table, mem_prompt, g_mix, w_in, b_f, g_sgu_v, w_s, b_s, g_att_out, g_sgu_out, w_o,
           g_cross, g_mem, w_cq, w_ck, w_cv, w_co, g_ffn, w_up, conv_w, conv_b, w_down, g_final):
    depth = w_in.shape[0]
    bp, seq, _ = x_prompt.shape
    bs = x_sample.shape[0]
    assert depth == 1 and x_sample.shape[1] == 1 and seq % PROMPT_TILE == 0
    n_phys = cache_k.shape[1]
    tm = PROMPT_TILE
    tpb = seq // tm
    nq = seq // ATT_TILE
    row1 = lambda a: a.reshape(1, -1)

    xp = x_prompt.reshape(bp * seq, D_MODEL)
    xs = x_sample.reshape(bs, D_MODEL)
    tri = (lax.broadcasted_iota(jnp.int32, (tm, tm), 0)
           <= lax.broadcasted_iota(jnp.int32, (tm, tm), 1)).astype(BF16)
    gfin = row1(g_final)
    outs = [[] for _ in range(11)]
    for l in range(depth):
        wl = w_in[l]
        w_r = jnp.concatenate([wl[:, :O_F], wl[:, O_U:],
                               jnp.pad(wl[:, O_F:O_U], ((0, 0), (0, LANES - ATT_HEADS)))], axis=1).astype(BF16)
        bf128 = jnp.pad(b_f[l], (0, LANES - ATT_HEADS)).reshape(1, LANES)
        wo_b, wcq_b, wck_b, wcv_b, wco_b = (w[l].astype(BF16) for w in (w_o, w_cq, w_ck, w_cv, w_co))
        wu_b, wd_b = w_up[l].astype(BF16), w_down[l].astype(BF16)
        gmix, gz, ga, gs = row1(g_mix[l]), row1(g_sgu_v[l]), row1(g_att_out[l]), row1(g_sgu_out[l])
        gc, gm, gff = row1(g_cross[l]), row1(g_mem[l]), row1(g_ffn[l])
        cb = row1(conv_b[l])
        bs_full = jnp.repeat(b_s[l].T, ATT_HEAD_DIM, axis=1)

        qt, kt, vt, kb, aug, lft_p, u, zb = _inproj(xp, gmix, w_r, bf128, gz, tri, prompt=True,
                                                   tm=tm, tiles_per_batch=tpb)
        att = _fox_prompt(qt, kb, aug, vt, batch=bp, seq=seq, tq=ATT_TILE)
        x1 = _merge_prompt(att, u, zb, xp, w_s[l], bs_full, ga, gs, wo_b, tm=tm)
        mk, mv, mkb, mvb = _memkv(mem_prompt.reshape(bp * N_MEM, D_MODEL), gm, wck_b, wcv_b, tm=tm)
        x2 = _cross_prompt(x1, mkb, mvb, gc, wcq_b, wco_b, batch=bp, seq=seq, tm=tm)
        xp, cg, cv = _ffn_prompt(x2, gff, wu_b, conv_w[l], cb, wd_b, gfin, batch=bp, seq=seq, tm=tm)
        heads_t = lambda a: jnp.transpose(a.reshape(bp, ATT_HEADS, ATT_HEAD_DIM, seq), (0, 3, 1, 2))
        outs[0].append(heads_t(kt))
        outs[1].append(heads_t(vt))
        outs[2].append(jnp.transpose(lft_p, (0, 2, 1)))
        outs[3].append(mk.reshape(bp, N_MEM, MEM_HEADS, MEM_HEAD_DIM))
        outs[4].append(mv.reshape(bp, N_MEM, MEM_HEADS, MEM_HEAD_DIM))
        outs[5].append(jnp.concatenate([cg, cv], axis=-1))

        qs, ks, vs, lfs, us, zs = _inproj(xs, gmix, w_r, bf128, gz, None, prompt=False,
                                          tm=bs, tiles_per_batch=1)
        kt = jnp.transpose(cache_k[l], (0, 2, 3, 1)).reshape(n_phys, HD_ROWS, PAGE_SIZE)
        vt = jnp.transpose(cache_v[l], (0, 2, 3, 1)).reshape(n_phys, HD_ROWS, PAGE_SIZE)
        lft = jnp.transpose(cache_logf[l], (0, 2, 1))
        col = lambda a: a.reshape(bs, -1, 1)
        att_s = _fox_decode(page_table, col(qs), col(ks), col(vs), col(lfs), kt, vt, lft).reshape(bs, ATT_WIDTH)
        w0 = row1(jnp.repeat(w_s[l][:, 0, 0], ATT_HEAD_DIM))
        b0 = row1(jnp.repeat(b_s[l][:, 0], ATT_HEAD_DIM))
        x1s, qcs = _merge_sample(att_s, us, zs, xs, w0, b0, ga, gs, wo_b, gc, wcq_b)
        oc = _cross_sample(qcs.reshape(bs, 1, D_MODEL),
                           cache_mem_k[l].reshape(bs, N_MEM, D_MODEL),
                           cache_mem_v[l].reshape(bs, N_MEM, D_MODEL)).reshape(bs, D_MODEL)
        prev0, prev1 = state_conv[l][:, 0, :], state_conv[l][:, 1, :]
        xs, hg, hv = _ffn_sample(oc, x1s, wco_b, gff, wu_b, conv_w[l], cb, prev0, prev1, wd_b, gfin)
        h_new = jnp.concatenate([hg, hv], axis=-1)
        outs[6].append(ks.reshape(bs, 1, ATT_HEADS, ATT_HEAD_DIM))
        outs[7].append(vs.reshape(bs, 1, ATT_HEADS, ATT_HEAD_DIM))
        outs[8].append(lfs.reshape(bs, 1, ATT_HEADS))
        outs[9].append(zs.reshape(bs, 1, ATT_HEADS, ATT_HEAD_DIM))
        outs[10].append(jnp.stack([prev1, h_new], axis=1))

    y_prompt = xp.reshape(bp, seq, D_MODEL)
    y_sample = xs.reshape(bs, 1, D_MODEL)
    return (y_prompt, y_sample) + tuple(jnp.stack(o) for o in outs)
```

```python
import functools
import math

import jax
import jax.numpy as jnp
from jax import lax
from jax.experimental import pallas as pl
from jax.experimental.pallas import tpu as pltpu

D_MODEL = 1024
PAGE_SIZE = 128
CHUNK = 128
ATT_HEADS = 8
ATT_HEAD_DIM = 64
ATT_WIDTH = ATT_HEADS * ATT_HEAD_DIM
SGU_WIDTH = 512
O_F = 3 * ATT_WIDTH
O_U = O_F + ATT_HEADS
N_MEM = 256
MEM_HEADS = 4
MEM_HEAD_DIM = D_MODEL // MEM_HEADS
D_FF = 2816
CONV_W = 3
EPS = 1e-6

LOG2E = math.log2(math.e)
LANES = 128
W_IN_COLS = 5 * 512 + LANES
VMEM_LIMIT = 56 * 1024 * 1024

F32 = jnp.float32
BF16 = jnp.bfloat16


def _rms(x, g):
    return x * lax.rsqrt(jnp.mean(x * x, axis=-1, keepdims=True) + EPS) * g


def _gelu(x):
    c = math.sqrt(2.0 / math.pi)
    return x * (0.5 * (1.0 + jnp.tanh(c * (x + 0.044715 * (x * x * x)))))


def _log_sigmoid(x):
    return jnp.minimum(x, 0.0) - jnp.log1p(jnp.exp(-jnp.abs(x)))


def _silu(x):
    return x * (0.5 * (1.0 + jnp.tanh(0.5 * x)))


def _dot(a, b):
    return jnp.dot(a, b, preferred_element_type=F32)


def _dot_nt(a, b):
    return lax.dot_general(a, b, (((1,), (1,)), ((), ())), preferred_element_type=F32)


def _split3(x):
    hi = x.astype(BF16)
    r = x - hi.astype(F32)
    mid = r.astype(BF16)
    lo = (r - mid.astype(F32)).astype(BF16)
    return hi, mid, lo


def _params(*sem):
    return pltpu.CompilerParams(dimension_semantics=sem, vmem_limit_bytes=VMEM_LIMIT)


def _const(shape):
    return pl.BlockSpec(shape, lambda *_: (0,) * len(shape))


def _inproj_body(*refs, prompt, tm, tiles_per_batch):
    if prompt:
        (x_ref, g_ref, w_ref, bf_ref, gz_ref, tri_ref,
         qt_ref, kt_ref, vt_ref, kb_ref, aug_ref, lft_ref, u_ref, zb_ref, carry_ref) = refs
    else:
        (x_ref, g_ref, w_ref, bf_ref, gz_ref,
         q_ref, k_ref, v_ref, lf_ref, u_ref, z_ref) = refs
    xn = _rms(x_ref[...], g_ref[...]).astype(BF16)

    def proj(a, b):
        return _dot(xn, w_ref[:, a:b])

    q = proj(0, 512)
    k = proj(512, 1024)
    v = proj(1024, 1536)
    u_ref[...] = _gelu(proj(1536, 2048))
    z = _rms(_gelu(proj(2048, 2560)), gz_ref[...])
    lf = _log_sigmoid(proj(2560, W_IN_COLS) + bf_ref[...])
    if not prompt:
        q_ref[...] = q
        k_ref[...] = k
        v_ref[...] = v
        lf_ref[...] = lf[:, :ATT_HEADS]
        z_ref[...] = z
        return
    qt_ref[...] = (q * (ATT_HEAD_DIM ** -0.5 * LOG2E)).T.astype(BF16)
    kt_ref[0] = k.T
    vt_ref[0] = v.T
    kb_ref[...] = k.astype(BF16)
    zb_ref[...] = z.astype(BF16)

    @pl.when(pl.program_id(0) % tiles_per_batch == 0)
    def _():
        carry_ref[...] = jnp.zeros_like(carry_ref)

    lft = lf.T[:ATT_HEADS, :]
    lft_ref[0] = lft
    tri = tri_ref[...]
    hi, mid, lo = _split3(lft)
    c = _dot(hi, tri) + _dot(mid, tri) + _dot(lo, tri) + carry_ref[:, 0:1]
    carry_ref[...] = jnp.broadcast_to(c[:, tm - 1:tm], carry_ref.shape)
    terms = [t.astype(F32) for t in _split3(c * (-LOG2E))]
    pad = jnp.zeros((LANES - 3 * ATT_HEADS, tm), F32)
    aug_ref[...] = jnp.concatenate(terms + [pad], axis=0).T.astype(BF16)


def _inproj(x2d, g, w_r, bf128, gz, tri, *, prompt, tm, tiles_per_batch):
    rows = x2d.shape[0]
    row = lambda n: pl.BlockSpec((tm, n), lambda i: (i, 0))
    in_specs = [row(D_MODEL), _const((1, D_MODEL)), _const((D_MODEL, W_IN_COLS)),
                _const((1, LANES)), _const((1, SGU_WIDTH))]
    args = [x2d, g, w_r, bf128, gz]
    sds = jax.ShapeDtypeStruct
    if prompt:
        in_specs.append(_const((tm, tm)))
        args.append(tri)
        nb = rows // (tm * tiles_per_batch)
        seq = tm * tiles_per_batch
        tpb = tiles_per_batch
        tl = lambda n: pl.BlockSpec((1, n, tm), lambda i: (i // tpb, 0, i % tpb))
        out_shape = [sds((512, rows), BF16), sds((nb, 512, seq), F32), sds((nb, 512, seq), F32),
                     sds((rows, 512), BF16), sds((rows, LANES), BF16), sds((nb, ATT_HEADS, seq), F32),
                     sds((rows, 512), F32), sds((rows, 512), BF16)]
        out_specs = [pl.BlockSpec((512, tm), lambda i: (0, i)), tl(512), tl(512),
                     row(512), row(LANES), tl(ATT_HEADS), row(512), row(512)]
        scratch = [pltpu.VMEM((ATT_HEADS, LANES), F32)]
    else:
        out_shape = [sds((rows, 512), F32)] * 3 + [sds((rows, ATT_HEADS), F32)] + [sds((rows, 512), F32)] * 2
        out_specs = [row(512)] * 3 + [row(ATT_HEADS), row(512), row(512)]
        scratch = []
    return pl.pallas_call(
        functools.partial(_inproj_body, prompt=prompt, tm=tm, tiles_per_batch=tiles_per_batch),
        grid=(rows // tm,), in_specs=in_specs, out_specs=out_specs, out_shape=out_shape,
        scratch_shapes=scratch, compiler_params=_params("arbitrary"),
        name="inproj_prompt" if prompt else "inproj_sample")(*args)


def _fox_body(qt_ref, kb_ref, aug_ref, vt_ref, o_ref, vb_ref, ka_ref, s_ref, p_ref, *, seq, tq):
    nq = seq // tq
    pair = pl.program_id(1)
    vb_ref[...] = vt_ref[0].astype(BF16)
    ka_ref[:, 0:LANES] = kb_ref[...]
    ka_ref[:, LANES:] = aug_ref[...]
    key = lax.broadcasted_iota(jnp.int32, (tq, tq), 0)
    qry = lax.broadcasted_iota(jnp.int32, (tq, tq), 1)
    causal = key <= qry
    sub = lax.broadcasted_iota(jnp.int32, (LANES, tq), 0)
    zero = jnp.zeros((ATT_HEAD_DIM, tq), BF16)
    picks = []
    for hh in range(2):
        h = 2 * pair + hh
        hit = (sub == h) | (sub == h + ATT_HEADS) | (sub == h + 2 * ATT_HEADS)
        picks.append(jnp.where(hit, 1.0, 0.0).astype(BF16))
    causal2 = jnp.concatenate([causal, causal], axis=1)
    colmax = lambda a: jnp.max(a, axis=0, keepdims=True)
    colsum = lambda a: jnp.sum(a, axis=0, keepdims=True)

    def scores(i):
        q0, q1 = i * tq, (i + 1) * tq
        qa = qt_ref[0:ATT_HEAD_DIM, q0:q1]
        qb = qt_ref[ATT_HEAD_DIM:, q0:q1]
        rhs = jnp.concatenate([jnp.concatenate([qa, zero, picks[0]], axis=0),
                               jnp.concatenate([zero, qb, picks[1]], axis=0)], axis=1)
        s_ref[i % 2, 0:q1, :] = _dot(ka_ref[0:q1, :], rhs)

    def finish(i):
        q0, q1 = i * tq, (i + 1) * tq
        sv = s_ref.at[i % 2]
        pv = p_ref.at[i % 2]
        tail = jnp.where(causal2, sv[q0:q1, :], -jnp.inf)
        m = colmax(tail)
        if i > 0:
            m = jnp.maximum(m, colmax(sv[0:q0, :]))
        pt = jnp.exp2(tail - m)
        l = colsum(pt)
        pv[q0:q1, :] = pt.astype(BF16)
        if i > 0:
            ph = jnp.exp2(sv[0:q0, :] - m)
            l = l + colsum(ph)
            pv[0:q0, :] = ph.astype(BF16)
        halves = []
        for hh in range(2):
            r0, r1 = hh * ATT_HEAD_DIM, (hh + 1) * ATT_HEAD_DIM
            c0, c1 = hh * tq, (hh + 1) * tq
            halves.append(_dot(vb_ref[r0:r1, 0:q1], pv[0:q1, c0:c1]) / l[:, c0:c1])
        o_ref[q0:q1, :] = jnp.concatenate(halves, axis=0).T

    scores(0)
    for i in range(nq):
        if i + 1 < nq:
            scores(i + 1)
        finish(i)


def _fox_prompt(qt, kb, aug, vt, *, batch, seq, tq):
    pairs = ATT_HEADS // 2
    return pl.pallas_call(
        functools.partial(_fox_body, seq=seq, tq=tq),
        grid=(batch, pairs),
        in_specs=[pl.BlockSpec((LANES, seq), lambda b, p: (p, b)),
                  pl.BlockSpec((seq, LANES), lambda b, p: (b, p)),
                  pl.BlockSpec((seq, LANES), lambda b, p: (b, 0)),
                  pl.BlockSpec((1, LANES, seq), lambda b, p: (b, p, 0))],
        out_specs=pl.BlockSpec((seq, LANES), lambda b, p: (b, p)),
        out_shape=jax.ShapeDtypeStruct((batch * seq, ATT_WIDTH), F32),
        scratch_shapes=[pltpu.VMEM((LANES, seq), BF16), pltpu.VMEM((seq, 2 * LANES), BF16),
                        pltpu.VMEM((2, seq, 2 * tq), F32), pltpu.VMEM((2, seq, 2 * tq), BF16)],
        compiler_params=_params("arbitrary", "arbitrary"),
        name="fox_prompt")(qt, kb, aug, vt)


def _merge_body(att_ref, u_ref, z_ref, x_ref, ws_ref, bs_ref, ga_ref, gs_ref, wo_ref,
                o_ref, sgu_ref, *, tm):
    rowi = lax.broadcasted_iota(jnp.int32, (CHUNK, CHUNK), 0)
    coli = lax.broadcasted_iota(jnp.int32, (CHUNK, CHUNK), 1)
    tril = coli <= rowi
    wm = [jnp.where(tril, ws_ref[h], 0.0).astype(BF16) for h in range(ATT_HEADS)]
    first = lax.broadcasted_iota(jnp.int32, (CHUNK, LANES), 1) < ATT_HEAD_DIM
    for c in range(tm // CHUNK):
        r0, r1 = c * CHUNK, (c + 1) * CHUNK
        for p in range(SGU_WIDTH // LANES):
            c0, c1 = p * LANES, (p + 1) * LANES
            zp = z_ref[r0:r1, c0:c1]
            mixed = jnp.where(first, _dot(wm[2 * p], zp), _dot(wm[2 * p + 1], zp)) + bs_ref[:, c0:c1]
            sgu_ref[r0:r1, c0:c1] = u_ref[r0:r1, c0:c1] * mixed
    an = _rms(att_ref[...], ga_ref[...]).astype(BF16)
    sn = _rms(sgu_ref[...], gs_ref[...]).astype(BF16)
    o_ref[...] = x_ref[...] + _dot(an, wo_ref[0:ATT_WIDTH, :]) + _dot(sn, wo_ref[ATT_WIDTH:, :])


def _merge_prompt(att, u, zb, x2d, ws, bs_full, ga, gs, wo, *, tm):
    rows = x2d.shape[0]
    row = lambda n: pl.BlockSpec((tm, n), lambda i: (i, 0))
    return pl.pallas_call(
        functools.partial(_merge_body, tm=tm),
        grid=(rows // tm,),
        in_specs=[row(512), row(512), row(512), row(D_MODEL),
                  _const((ATT_HEADS, CHUNK, CHUNK)), _const((CHUNK, SGU_WIDTH)),
                  _const((1, 512)), _const((1, 512)), _const((D_MODEL, D_MODEL))],
        out_specs=row(D_MODEL),
        out_shape=jax.ShapeDtypeStruct((rows, D_MODEL), F32),
        scratch_shapes=[pltpu.VMEM((tm, SGU_WIDTH), F32)],
        compiler_params=_params("arbitrary"), name="merge_prompt")(att, u, zb, x2d, ws, bs_full, ga, gs, wo)


def _memkv_body(m_ref, g_ref, wk_ref, wv_ref, k_ref, v_ref, kb_ref, vb_ref):
    mn = _rms(m_ref[...], g_ref[...]).astype(BF16)
    k = _dot(mn, wk_ref[...])
    v = _dot(mn, wv_ref[...])
    k_ref[...] = k
    v_ref[...] = v
    kb_ref[...] = k.astype(BF16)
    vb_ref[...] = v.astype(BF16)


def _memkv(mem2d, g, wk, wv, *, tm):
    rows = mem2d.shape[0]
    row = pl.BlockSpec((tm, D_MODEL), lambda i: (i, 0))
    sds = jax.ShapeDtypeStruct
    return pl.pallas_call(
        _memkv_body, grid=(rows // tm,),
        in_specs=[row, _const((1, D_MODEL)), _const((D_MODEL, D_MODEL)), _const((D_MODEL, D_MODEL))],
        out_specs=[row] * 4,
        out_shape=[sds((rows, D_MODEL), F32)] * 2 + [sds((rows, D_MODEL), BF16)] * 2,
        compiler_params=_params("arbitrary"), name="memkv")(mem2d, g, wk, wv)


def _cross_body(x_ref, mk_ref, mv_ref, g_ref, wq_ref, wo_ref, o_ref, oc_ref):
    x = x_ref[...]
    xn = _rms(x, g_ref[...]).astype(BF16)
    q = (_dot(xn, wq_ref[...]) * (MEM_HEAD_DIM ** -0.5)).astype(BF16)
    for h in range(MEM_HEADS):
        c0, c1 = h * MEM_HEAD_DIM, (h + 1) * MEM_HEAD_DIM
        s = _dot_nt(q[:, c0:c1], mk_ref[:, c0:c1])
        p = jnp.exp(s - jnp.max(s, axis=1, keepdims=True))
        l = jnp.sum(p, axis=1, keepdims=True)
        oc_ref[:, c0:c1] = (_dot(p.astype(BF16), mv_ref[:, c0:c1]) / l).astype(BF16)
    o_ref[...] = x + _dot(oc_ref[...], wo_ref[...])


def _cross_prompt(x2d, mkb, mvb, g, wq, wo, *, batch, seq, tm):
    nt = seq // tm
    row = pl.BlockSpec((tm, D_MODEL), lambda b, i: (b * nt + i, 0))
    mem = pl.BlockSpec((N_MEM, D_MODEL), lambda b, i: (b, 0))
    return pl.pallas_call(
        _cross_body, grid=(batch, nt),
        in_specs=[row, mem, mem, _const((1, D_MODEL)), _const((D_MODEL, D_MODEL)), _const((D_MODEL, D_MODEL))],
        out_specs=row, out_shape=jax.ShapeDtypeStruct(x2d.shape, F32),
        scratch_shapes=[pltpu.VMEM((tm, D_MODEL), BF16)],
        compiler_params=_params("arbitrary", "arbitrary"), name="cross_prompt")(x2d, mkb, mvb, g, wq, wo)


HALO = 16
FF_TILE = 256


def _ffn_body(x_ref, xp_ref, g_ref, wu_ref, cw_ref, cb_ref, wd_ref, gf_ref,
              y_ref, cg_ref, cv_ref, hg_ref, hv_ref, acc_ref, xn_ref, *, tm, tiles_per_batch):
    first = pl.program_id(0) % tiles_per_batch == 0
    x = x_ref[...]
    g = g_ref[...]
    xn_ref[0:HALO, :] = _rms(xp_ref[...], g).astype(BF16)
    xn_ref[HALO:, :] = _rms(x, g).astype(BF16)
    acc_ref[...] = x
    nchunk = D_FF // FF_TILE
    sides = ((0, hg_ref, cg_ref), (D_FF, hv_ref, cv_ref))

    def up(c):
        a, b = c * FF_TILE, (c + 1) * FF_TILE
        for off, h_ref, _ in sides:
            w = wu_ref[:, off + a:off + b]
            hp = _dot(xn_ref[0:HALO, :], w)
            h_ref[c % 2, 0:HALO, :] = jnp.where(first, jnp.zeros_like(hp), hp)
            h_ref[c % 2, HALO:, :] = _dot(xn_ref[HALO:, :], w)

    def down(c):
        a, b = c * FF_TILE, (c + 1) * FF_TILE
        halves = []
        for off, h_ref, c_ref in sides:
            h = h_ref.at[c % 2]
            cw = cw_ref[:, off + a:off + b]
            halves.append(cb_ref[:, off + a:off + b]
                          + cw[0:1] * h[HALO - 2:HALO - 2 + tm, :]
                          + cw[1:2] * h[HALO - 1:HALO - 1 + tm, :]
                          + cw[2:3] * h[HALO:HALO + tm, :])
            c_ref[0, :, a:b] = h[HALO + tm - 2:HALO + tm, :]
        act = (_silu(halves[0]) * halves[1]).astype(BF16)
        acc_ref[...] += _dot(act, wd_ref[a:b, :])

    up(0)
    for c in range(nchunk):
        if c + 1 < nchunk:
            up(c + 1)
        down(c)
    y_ref[...] = _rms(acc_ref[...], gf_ref[...])


def _ffn_prompt(x2d, g, wu, cw, cb, wd, gf, *, batch, seq, tm):
    rows = x2d.shape[0]
    tpb = seq // tm
    hb = tm // HALO
    single = dict(pipeline_mode=pl.Buffered(1))
    sds = jax.ShapeDtypeStruct
    conv_spec = pl.BlockSpec((1, CONV_W - 1, D_FF), lambda i: (i // tpb, 0, 0))
    return pl.pallas_call(
        functools.partial(_ffn_body, tm=tm, tiles_per_batch=tpb),
        grid=(rows // tm,),
        in_specs=[pl.BlockSpec((tm, D_MODEL), lambda i: (i, 0)),
                  pl.BlockSpec((HALO, D_MODEL), lambda i: (jnp.maximum(i * hb - 1, 0), 0)),
                  _const((1, D_MODEL)),
                  pl.BlockSpec((D_MODEL, 2 * D_FF), lambda i: (0, 0), **single),
                  _const((CONV_W, 2 * D_FF)), _const((1, 2 * D_FF)),
                  pl.BlockSpec((D_FF, D_MODEL), lambda i: (0, 0), **single),
                  _const((1, D_MODEL))],
        out_specs=[pl.BlockSpec((tm, D_MODEL), lambda i: (i, 0)), conv_spec, conv_spec],
        out_shape=[sds((rows, D_MODEL), F32), sds((batch, CONV_W - 1, D_FF), F32),
                   sds((batch, CONV_W - 1, D_FF), F32)],
        scratch_shapes=[pltpu.VMEM((2, HALO + tm, FF_TILE), F32), pltpu.VMEM((2, HALO + tm, FF_TILE), F32),
                        pltpu.VMEM((tm, D_MODEL), F32), pltpu.VMEM((HALO + tm, D_MODEL), BF16)],
        compiler_params=_params("arbitrary"), name="ffn_prompt")(x2d, x2d, g, wu, cw, cb, wd, gf)


DEC_PAGES_PER_STEP = 8
HD_ROWS = ATT_HEADS * ATT_HEAD_DIM


def _head_rows(x):
    n = x.shape[-1]
    return jnp.broadcast_to(x[:, None, :], (ATT_HEADS, ATT_HEAD_DIM, n)).reshape(HD_ROWS, n)


def _head_sum(x):
    n = x.shape[-1]
    return jnp.sum(x.reshape(ATT_HEADS, ATT_HEAD_DIM, n), axis=1)


def _decode_body(pt_ref, q_ref, kn_ref, vn_ref, lfn_ref, u2_ref, kt_hbm, vt_hbm, lft_hbm, o_ref,
                 kbuf, vbuf, lfbuf, sem, m_ref, l_ref, acc_ref, carry_ref, *, npg, steps):
    b = pl.program_id(0)
    last_b = pl.num_programs(0) - 1

    def copies(step, slot):
        out = []
        for i in range(npg):
            pid = pt_ref[step * npg + i]
            out.append((pltpu.make_async_copy(kt_hbm.at[pid], kbuf.at[slot, i], sem.at[slot, 0]), i % 2))
            out.append((pltpu.make_async_copy(vt_hbm.at[pid], vbuf.at[slot, i], sem.at[slot, 1]), (i + 1) % 2))
            out.append((pltpu.make_async_copy(lft_hbm.at[pid], lfbuf.at[slot, i], sem.at[slot, 2]), i % 2))
        return out

    def start(step, slot):
        for cp, queue in copies(step, slot):
            cp.start(priority=queue)

    def wait(step, slot):
        for cp, _ in copies(step, slot):
            cp.wait()

    @pl.when(b == 0)
    def _():
        start(0, 0)

    qcol = q_ref[0] * (ATT_HEAD_DIM ** -0.5)
    lane8 = lax.broadcasted_iota(jnp.int32, (ATT_HEADS, LANES), 1)
    s_new = _head_sum(qcol * kn_ref[0])
    m_ref[...] = jnp.broadcast_to(s_new, m_ref.shape)
    l_ref[...] = jnp.where(lane8 == 0, 1.0, 0.0)
    lane = lax.broadcasted_iota(jnp.int32, (HD_ROWS, LANES), 1)
    acc_ref[...] = jnp.where(lane == 0, jnp.broadcast_to(vn_ref[0], (HD_ROWS, LANES)), 0.0)
    carry_ref[...] = jnp.broadcast_to(lfn_ref[0], carry_ref.shape)
    qrep = jnp.broadcast_to(qcol, (HD_ROWS, LANES))

    def compute(slot):
        lfs = [lfbuf[slot, i] for i in range(npg)]
        terms = [t.astype(F32) for lf in lfs for t in _split3(lf)]
        scan = _dot(jnp.concatenate(terms, axis=0).astype(BF16), u2_ref[...])
        carry = carry_ref[...]
        scores = []
        for i in range(npg):
            r = 3 * ATT_HEADS * i
            both = scan[r:r + 8] + scan[r + 8:r + 16] + scan[r + 16:r + 24]
            inc, total = both[:, :LANES], both[:, LANES:]
            scores.append(_head_sum(kbuf[slot, i] * qrep) + (carry + (inc - lfs[i])))
            carry = carry + total
        carry_ref[...] = carry
        mx = functools.reduce(jnp.maximum, scores)
        m_old = m_ref[...]
        m_new = jnp.maximum(m_old, jnp.broadcast_to(jnp.max(mx, axis=1, keepdims=True), m_old.shape))
        alpha = jnp.exp(m_old - m_new)
        m_ref[...] = m_new
        probs = [jnp.exp(s - m_new) for s in scores]
        l_ref[...] = alpha * l_ref[...] + functools.reduce(jnp.add, probs)
        acc = acc_ref[...] * _head_rows(alpha)
        for i in range(npg):
            acc = acc + vbuf[slot, i] * _head_rows(probs[i])
        acc_ref[...] = acc

    for s in range(steps):
        slot = s % 2
        step = b * steps + s
        if s + 1 < steps:
            start(step + 1, 1 - slot)
        else:
            @pl.when(b < last_b)
            def _():
                start(step + 1, 1 - slot)
        wait(step, slot)
        compute(slot)

    l = jnp.sum(l_ref[...], axis=1, keepdims=True)
    o_ref[0] = jnp.sum(acc_ref[...], axis=1, keepdims=True) / _head_rows(l)


def _fox_decode(page_table, qcol, kncol, vncol, lfncol, kt, vt, lft):
    nb, npages = page_table.shape
    npg = DEC_PAGES_PER_STEP
    steps = npages // npg
    assert steps % 2 == 0 and steps * npg == npages
    order = page_table[:, ::-1].reshape(-1)
    tok = lax.broadcasted_iota(jnp.int32, (PAGE_SIZE, PAGE_SIZE), 0)
    u2 = jnp.concatenate([(tok >= tok.T).astype(BF16), jnp.ones((PAGE_SIZE, PAGE_SIZE), BF16)], axis=1)
    col = lambda n: pl.BlockSpec((1, n, 1), lambda b, pt: (b, 0, 0))
    hbm = pl.BlockSpec(memory_space=pl.ANY)
    grid_spec = pltpu.PrefetchScalarGridSpec(
        num_scalar_prefetch=1, grid=(nb,),
        in_specs=[col(HD_ROWS), col(HD_ROWS), col(HD_ROWS), col(ATT_HEADS), _const((PAGE_SIZE, 2 * PAGE_SIZE)),
                  hbm, hbm, hbm],
        out_specs=col(HD_ROWS),
        scratch_shapes=[pltpu.VMEM((2, npg, HD_ROWS, PAGE_SIZE), F32), pltpu.VMEM((2, npg, HD_ROWS, PAGE_SIZE), F32),
                        pltpu.VMEM((2, npg, ATT_HEADS, PAGE_SIZE), F32), pltpu.SemaphoreType.DMA((2, 3)),
                        pltpu.VMEM((ATT_HEADS, LANES), F32), pltpu.VMEM((ATT_HEADS, LANES), F32),
                        pltpu.VMEM((HD_ROWS, LANES), F32), pltpu.VMEM((ATT_HEADS, LANES), F32)])
    return pl.pallas_call(
        functools.partial(_decode_body, npg=npg, steps=steps), grid_spec=grid_spec,
        out_shape=jax.ShapeDtypeStruct((nb, HD_ROWS, 1), F32),
        compiler_params=_params("arbitrary"), name="fox_decode")(
            order, qcol, kncol, vncol, lfncol, u2, kt, vt, lft)


def _merge_sample_body(att_ref, u_ref, z_ref, x_ref, w0_ref, b0_ref, ga_ref, gs_ref, wo_ref,
                       gc_ref, wq_ref, x1_ref, q_ref):
    sgu = u_ref[...] * (w0_ref[...] * z_ref[...] + b0_ref[...])
    an = _rms(att_ref[...], ga_ref[...]).astype(BF16)
    sn = _rms(sgu, gs_ref[...]).astype(BF16)
    x1 = x_ref[...] + _dot(an, wo_ref[0:ATT_WIDTH, :]) + _dot(sn, wo_ref[ATT_WIDTH:, :])
    x1_ref[...] = x1
    q_ref[...] = _dot(_rms(x1, gc_ref[...]).astype(BF16), wq_ref[...]) * (MEM_HEAD_DIM ** -0.5)


def _merge_sample(att, u, z, x2d, w0, b0, ga, gs, wo, gc, wq):
    sds = jax.ShapeDtypeStruct
    return pl.pallas_call(
        _merge_sample_body,
        out_shape=[sds(x2d.shape, F32), sds(x2d.shape, F32)],
        compiler_params=pltpu.CompilerParams(vmem_limit_bytes=VMEM_LIMIT),
        name="merge_sample")(att, u, z, x2d, w0, b0, ga, gs, wo, gc, wq)


def _cross_sample_body(q_ref, mk_ref, mv_ref, o_ref):
    q = q_ref[0]
    rowi = lax.broadcasted_iota(jnp.int32, (8, D_MODEL), 0)
    head = lax.broadcasted_iota(jnp.int32, (8, D_MODEL), 1) // MEM_HEAD_DIM
    own = rowi == head
    qbd = jnp.where(own, jnp.broadcast_to(q, (8, D_MODEL)), 0.0).astype(BF16)
    s = _dot_nt(qbd, mk_ref[0].astype(BF16))
    p = jnp.exp(s - jnp.max(s, axis=1, keepdims=True))
    l = jnp.sum(p, axis=1, keepdims=True)
    o = _dot(p.astype(BF16), mv_ref[0].astype(BF16)) / l
    o_ref[0] = jnp.sum(jnp.where(own, o, 0.0), axis=0, keepdims=True)


def _cross_sample(q3, mk, mv):
    nb = q3.shape[0]
    vec = pl.BlockSpec((1, 1, D_MODEL), lambda b: (b, 0, 0))
    mem = pl.BlockSpec((1, N_MEM, D_MODEL), lambda b: (b, 0, 0))
    return pl.pallas_call(
        _cross_sample_body, grid=(nb,), in_specs=[vec, mem, mem], out_specs=vec,
        out_shape=jax.ShapeDtypeStruct(q3.shape, F32),
        compiler_params=_params("arbitrary"), name="cross_sample")(q3, mk, mv)


def _ffn_sample_body(o_ref, x_ref, wco_ref, g_ref, wug_ref, wuv_ref, cwg_ref, cwv_ref, cbg_ref, cbv_ref,
                     p0g_ref, p0v_ref, p1g_ref, p1v_ref, wd_ref, gf_ref,
                     y_ref, hg_ref, hv_ref, x2_ref, xn_ref):
    j = pl.program_id(0)

    @pl.when(j == 0)
    def _():
        x2 = x_ref[...] + _dot(o_ref[...].astype(BF16), wco_ref[...])
        x2_ref[...] = x2
        xn_ref[...] = _rms(x2, g_ref[...]).astype(BF16)

    xn = xn_ref[...]
    hg = _dot(xn, wug_ref[...])
    hv = _dot(xn, wuv_ref[...])
    hg_ref[...] = hg
    hv_ref[...] = hv
    cwg = cwg_ref[...]
    cwv = cwv_ref[...]
    gate = cbg_ref[...] + cwg[0:1] * p0g_ref[...] + cwg[1:2] * p1g_ref[...] + cwg[2:3] * hg
    val = cbv_ref[...] + cwv[0:1] * p0v_ref[...] + cwv[1:2] * p1v_ref[...] + cwv[2:3] * hv
    x2_ref[...] += _dot((_silu(gate) * val).astype(BF16), wd_ref[...])

    @pl.when(j == pl.num_programs(0) - 1)
    def _():
        y_ref[...] = _rms(x2_ref[...], gf_ref[...])


def _ffn_sample(o, x1, wco, g, wu, cw, cb, prev0, prev1, wd, gf):
    nb = x1.shape[0]
    nf = D_FF // FF_TILE
    gcol = lambda r: pl.BlockSpec((r, FF_TILE), lambda j: (0, j))
    vcol = lambda r: pl.BlockSpec((r, FF_TILE), lambda j: (0, nf + j))
    sds = jax.ShapeDtypeStruct
    return pl.pallas_call(
        _ffn_sample_body, grid=(nf,),
        in_specs=[_const((nb, D_MODEL)), _const((nb, D_MODEL)), _const((D_MODEL, D_MODEL)), _const((1, D_MODEL)),
                  gcol(D_MODEL), vcol(D_MODEL), gcol(CONV_W), vcol(CONV_W), gcol(1), vcol(1),
                  gcol(nb), vcol(nb), gcol(nb), vcol(nb),
                  pl.BlockSpec((FF_TILE, D_MODEL), lambda j: (j, 0)), _const((1, D_MODEL))],
        out_specs=[_const((nb, D_MODEL)), gcol(nb), gcol(nb)],
        out_shape=[sds((nb, D_MODEL), F32), sds((nb, D_FF), F32), sds((nb, D_FF), F32)],
        scratch_shapes=[pltpu.VMEM((nb, D_MODEL), F32), pltpu.VMEM((nb, D_MODEL), BF16)],
        compiler_params=_params("arbitrary"), name="ffn_sample")(
            o, x1, wco, g, wu, wu, cw, cw, cb, cb, prev0, prev0, prev1, prev1, wd, gf)


PROMPT_TILE = 512
ATT_TILE = 256


def kernel(x_prompt, x_sample, cache_k, cache_v, cache_logf, cache_mem_k, cache_mem_v, state_conv,
           page_table, mem_prompt, g_mix, w_in, b_f, g_sgu_v, w_s, b_s, g_att_out, g_sgu_out, w_o,
           g_cross, g_mem, w_cq, w_ck, w_cv, w_co, g_ffn, w_up, conv_w, conv_b, w_down, g_final):
    depth = w_in.shape[0]
    bp, seq, _ = x_prompt.shape
    bs = x_sample.shape[0]
    assert depth == 1 and x_sample.shape[1] == 1 and seq % PROMPT_TILE == 0
    n_phys = cache_k.shape[1]
    tm = PROMPT_TILE
    tpb = seq // tm
    nq = seq // ATT_TILE
    row1 = lambda a: a.reshape(1, -1)

    xp = x_prompt.reshape(bp * seq, D_MODEL)
    xs = x_sample.reshape(bs, D_MODEL)
    tri = (lax.broadcasted_iota(jnp.int32, (tm, tm), 0)
           <= lax.broadcasted_iota(jnp.int32, (tm, tm), 1)).astype(BF16)
    gfin = row1(g_final)
    outs = [[] for _ in range(11)]
    for l in range(depth):
        wl = w_in[l]
        w_r = jnp.concatenate([wl[:, :O_F], wl[:, O_U:],
                               jnp.pad(wl[:, O_F:O_U], ((0, 0), (0, LANES - ATT_HEADS)))], axis=1).astype(BF16)
        bf128 = jnp.pad(b_f[l], (0, LANES - ATT_HEADS)).reshape(1, LANES)
        wo_b, wcq_b, wck_b, wcv_b, wco_b = (w[l].astype(BF16) for w in (w_o, w_cq, w_ck, w_cv, w_co))
        wu_b, wd_b = w_up[l].astype(BF16), w_down[l].astype(BF16)
        gmix, gz, ga, gs = row1(g_mix[l]), row1(g_sgu_v[l]), row1(g_att_out[l]), row1(g_sgu_out[l])
        gc, gm, gff = row1(g_cross[l]), row1(g_mem[l]), row1(g_ffn[l])
        cb = row1(conv_b[l])
        bs_full = jnp.repeat(b_s[l].T, ATT_HEAD_DIM, axis=1)

        qt, kt, vt, kb, aug, lft_p, u, zb = _inproj(xp, gmix, w_r, bf128, gz, tri, prompt=True,
                                                   tm=tm, tiles_per_batch=tpb)
        att = _fox_prompt(qt, kb, aug, vt, batch=bp, seq=seq, tq=ATT_TILE)
        x1 = _merge_prompt(att, u, zb, xp, w_s[l], bs_full, ga, gs, wo_b, tm=tm)
        mk, mv, mkb, mvb = _memkv(mem_prompt.reshape(bp * N_MEM, D_MODEL), gm, wck_b, wcv_b, tm=tm)
        x2 = _cross_prompt(x1, mkb, mvb, gc, wcq_b, wco_b, batch=bp, seq=seq, tm=tm)
        xp, cg, cv = _ffn_prompt(x2, gff, wu_b, conv_w[l], cb, wd_b, gfin, batch=bp, seq=seq, tm=tm)
        heads_t = lambda a: jnp.transpose(a.reshape(bp, ATT_HEADS, ATT_HEAD_DIM, seq), (0, 3, 1, 2))
        outs[0].append(heads_t(kt))
        outs[1].append(heads_t(vt))
        outs[2].append(jnp.transpose(lft_p, (0, 2, 1)))
        outs[3].append(mk.reshape(bp, N_MEM, MEM_HEADS, MEM_HEAD_DIM))
        outs[4].append(mv.reshape(bp, N_MEM, MEM_HEADS, MEM_HEAD_DIM))
        outs[5].append(jnp.concatenate([cg, cv], axis=-1))

        qs, ks, vs, lfs, us, zs = _inproj(xs, gmix, w_r, bf128, gz, None, prompt=False,
                                          tm=bs, tiles_per_batch=1)
        kt = jnp.transpose(cache_k[l], (0, 2, 3, 1)).reshape(n_phys, HD_ROWS, PAGE_SIZE)
        vt = jnp.transpose(cache_v[l], (0, 2, 3, 1)).reshape(n_phys, HD_ROWS, PAGE_SIZE)
        lft = jnp.transpose(cache_logf[l], (0, 2, 1))
        col = lambda a: a.reshape(bs, -1, 1)
        att_s = _fox_decode(page_table, col(qs), col(ks), col(vs), col(lfs), kt, vt, lft).reshape(bs, ATT_WIDTH)
        w0 = row1(jnp.repeat(w_s[l][:, 0, 0], ATT_HEAD_DIM))
        b0 = row1(jnp.repeat(b_s[l][:, 0], ATT_HEAD_DIM))
        x1s, qcs = _merge_sample(att_s, us, zs, xs, w0, b0, ga, gs, wo_b, gc, wcq_b)
        oc = _cross_sample(qcs.reshape(bs, 1, D_MODEL),
                           cache_mem_k[l].reshape(bs, N_MEM, D_MODEL),
                           cache_mem_v[l].reshape(bs, N_MEM, D_MODEL)).reshape(bs, D_MODEL)
        prev0, prev1 = state_conv[l][:, 0, :], state_conv[l][:, 1, :]
        xs, hg, hv = _ffn_sample(oc, x1s, wco_b, gff, wu_b, conv_w[l], cb, prev0, prev1, wd_b, gfin)
        h_new = jnp.concatenate([hg, hv], axis=-1)
        outs[6].append(ks.reshape(bs, 1, ATT_HEADS, ATT_HEAD_DIM))
        outs[7].append(vs.reshape(bs, 1, ATT_HEADS, ATT_HEAD_DIM))
        outs[8].append(lfs.reshape(bs, 1, ATT_HEADS))
        outs[9].append(zs.reshape(bs, 1, ATT_HEADS, ATT_HEAD_DIM))
        outs[10].append(jnp.stack([prev1, h_new], axis=1))

    y_prompt = xp.reshape(bp, seq, D_MODEL)
    y_sample = xs.reshape(bs, 1, D_MODEL)
    return (y_prompt, y_sample) + tuple(jnp.stack(o) for o in outs)
```
